```python
import math
import jax, jax.numpy as jnp
from jax import lax
import numpy as np

D_MODEL = 1024
BATCH = 2
SEQ = 16384
DEPTH = 4

N_MIXERS = 3
EPS = 1e-6
N_A = (DEPTH + 2) // 3
N_B = (DEPTH + 1) // 3
N_C = DEPTH // 3

HG_HEADS = 8
HG_DK = D_MODEL // HG_HEADS
HG_DV = D_MODEL // HG_HEADS
HG_CHUNK = 64
HG_IN = 4 * D_MODEL

SW_HEADS = 16
SW_KV_HEADS = 4
SW_GROUP = SW_HEADS // SW_KV_HEADS
SW_DH = 64
SW_WINDOW = 128
SW_BLOCK = 128
SW_QW = SW_HEADS * SW_DH
SW_KVW = SW_KV_HEADS * SW_DH
SW_IN = 2 * SW_QW + 2 * SW_KVW
ROPE_THETA = 10000.0
POS_OFFSET_MAX = 4096

GD_QK_HEADS = 8
GD_V_HEADS = 16
GD_DK = 128
GD_DV = 128
GD_CONV = 4
GD_CHUNK = 64
GD_QKW = GD_QK_HEADS * GD_DK
GD_VW = GD_V_HEADS * GD_DV
GD_QKV = 2 * GD_QKW + GD_VW
GD_IN = GD_QKV + GD_VW + 2 * GD_V_HEADS

kernel_name = 'hybrid_hgrn2_swa_sink_gdn_interleaved'


def rmsnorm(x, g):
    xf = x.astype(jnp.float32)
    y = xf * lax.rsqrt(jnp.mean(xf * xf, axis=-1, keepdims=True) + EPS)
    return (y * g.astype(jnp.float32)).astype(x.dtype)


def l2norm(x):
    xf = x.astype(jnp.float32)
    return xf * lax.rsqrt(jnp.sum(xf * xf, axis=-1, keepdims=True) + EPS)


def rope(x, ang):
    ang = ang.reshape(ang.shape[:2] + (1,) * (x.ndim - 3) + ang.shape[-1:])
    cos, sin = jnp.cos(ang), jnp.sin(ang)
    x1, x2 = jnp.split(x.astype(jnp.float32), 2, axis=-1)
    return jnp.concatenate([x1 * cos - x2 * sin, x2 * cos + x1 * sin], axis=-1).astype(x.dtype)


def to_chunks(a, chunk):
    B, T = a.shape[:2]
    a = a.reshape((B, T // chunk, chunk) + a.shape[2:])
    return jnp.moveaxis(jnp.moveaxis(a, 1, 0), 3, 2)


def from_chunks(a):
    a = jnp.moveaxis(jnp.moveaxis(a, 2, 3), 0, 1)
    return a.reshape((a.shape[0], a.shape[1] * a.shape[2]) + a.shape[3:])


def causal_conv(x, w):
    K, C = w.shape
    return lax.conv_general_dilated(x, w[:, None, :].astype(x.dtype), window_strides=(1,),
                                    padding=[(K - 1, 0)], dimension_numbers=('NWC', 'WIO', 'NWC'),
                                    feature_group_count=C)


def hgrn2_scan(q, k, v, log_f):
    B, T, H, dk = q.shape
    dv = v.shape[-1]
    causal = jnp.tril(jnp.ones((HG_CHUNK, HG_CHUNK), bool))

    def step(S, inp):
        q_, k_, v_, lf = inp
        b = jnp.cumsum(lf, axis=-2)
        diff = b[..., :, None, :] - b[..., None, :, :]
        dec = jnp.exp(jnp.where(causal[:, :, None], diff, -jnp.inf))
        A = jnp.einsum('bhtd,bhsd,bhtsd->bhts', q_, k_, dec)
        o = jnp.einsum('bhts,bhse->bhte', A, v_) + jnp.einsum('bhtd,bhde->bhte', q_ * jnp.exp(b), S)
        b_last = b[..., -1:, :]
        S = jnp.exp(b_last)[..., 0, :, None] * S + jnp.einsum('bhsd,bhse->bhde', k_ * jnp.exp(b_last - b), v_)
        return S, o

    xs = tuple(to_chunks(a.astype(jnp.float32), HG_CHUNK) for a in (q, k, v, log_f))
    S0 = jnp.zeros((B, H, dk, dv), jnp.float32)
    _, o = lax.scan(step, S0, xs)
    return from_chunks(o)


def hgrn2_mixer(h, w_in, w_out, onorm_g, lb):
    B, T, _ = h.shape
    q, f_pre, i_, z = jnp.split(h @ w_in, [D_MODEL, 2 * D_MODEL, 3 * D_MODEL], axis=-1)
    f_pre = f_pre.astype(jnp.float32)
    lb = lb.astype(jnp.float32)
    log_f = jnp.log(lb + (1.0 - lb) * jax.nn.sigmoid(f_pre))
    k = (1.0 - lb) * jax.nn.sigmoid(-f_pre)
    q = jax.nn.silu(q)
    shp = (B, T, HG_HEADS, HG_DK)
    o = hgrn2_scan(q.reshape(shp), k.reshape(shp), i_.reshape(B, T, HG_HEADS, HG_DV), log_f.reshape(shp))
    o = rmsnorm(o, onorm_g).reshape(B, T, HG_HEADS * HG_DV).astype(h.dtype)
    return (o * jax.nn.silu(z)) @ w_out


def swa_mixer(h, w_in, w_out, qn_g, kn_g, sinks, ang):
    B, T, _ = h.shape
    nb = T // SW_BLOCK
    q, k, v, z = jnp.split(h @ w_in, [SW_QW, SW_QW + SW_KVW, SW_QW + 2 * SW_KVW], axis=-1)
    q = rope(rmsnorm(q.reshape(B, T, SW_KV_HEADS, SW_GROUP, SW_DH), qn_g), ang)
    k = rope(rmsnorm(k.reshape(B, T, SW_KV_HEADS, SW_DH), kn_g), ang)
    v = v.reshape(B, T, SW_KV_HEADS, SW_DH)

    def band(a):
        cur = a.reshape(B, nb, SW_BLOCK, SW_KV_HEADS, SW_DH)
        prev = jnp.concatenate([jnp.zeros_like(cur[:, :1]), cur[:, :-1]], axis=1)
        return jnp.concatenate([prev, cur], axis=2)

    qb = q.reshape(B, nb, SW_BLOCK, SW_KV_HEADS, SW_GROUP, SW_DH)
    kb, vb = band(k), band(v)
    s = jnp.einsum('bnqhgd,bnkhd->bnhgqk', qb, kb).astype(jnp.float32) * (SW_DH ** -0.5)
    qi = jnp.arange(SW_BLOCK)[:, None]
    kj = jnp.arange(2 * SW_BLOCK)[None, :]
    rel = qi + SW_BLOCK - kj
    in_band = (rel >= 0) & (rel < SW_WINDOW)
    key_pos = jnp.arange(nb)[:, None] * SW_BLOCK + jnp.arange(2 * SW_BLOCK)[None, :] - SW_BLOCK
    mask = in_band[None] & (key_pos >= 0)[:, None, :]
    s = jnp.where(mask[None, :, None, None], s, -jnp.inf)
    sink = sinks.astype(jnp.float32).reshape(SW_KV_HEADS, SW_GROUP)[None, None, :, :, None, None]
    m = jnp.maximum(jnp.max(s, axis=-1, keepdims=True), sink)
    p = jnp.exp(s - m)
    p = p / (jnp.sum(p, axis=-1, keepdims=True) + jnp.exp(sink - m))
    o = jnp.einsum('bnhgqk,bnkhd->bnqhgd', p.astype(h.dtype), vb).reshape(B, T, SW_QW)
    return (o * jax.nn.silu(z)) @ w_out


def gated_delta_scan(q, k, v, beta, g):
    B, T, H, dk = q.shape
    dv = v.shape[-1]
    incl = jnp.tril(jnp.ones((GD_CHUNK, GD_CHUNK), bool))
    strict = jnp.tril(jnp.ones((GD_CHUNK, GD_CHUNK), jnp.float32), -1)
    eye = jnp.eye(GD_CHUNK, dtype=jnp.float32)

    def step(S, inp):
        q_, k_, v_, beta_, g_ = inp
        d = jnp.cumsum(g_, axis=-1)
        dec = jnp.exp(jnp.where(incl, d[..., :, None] - d[..., None, :], -jnp.inf))
        kb = k_ * beta_[..., None]
        A = jnp.einsum('bhid,bhjd->bhij', kb, k_) * dec * strict
        rhs = jnp.concatenate([v_ * beta_[..., None], kb * jnp.exp(d)[..., None]], axis=-1)
        X = lax.linalg.triangular_solve(A + eye, rhs, left_side=True, lower=True, unit_diagonal=True)
        u, w = X[..., :dv], X[..., dv:]
        v_new = u - jnp.einsum('bhid,bhde->bhie', w, S)
        qk = jnp.einsum('bhid,bhjd->bhij', q_, k_) * dec
        o = jnp.einsum('bhid,bhde->bhie', q_ * jnp.exp(d)[..., None], S) + jnp.einsum('bhij,bhje->bhie', qk, v_new)
        d_last = d[..., -1:]
        S = S * jnp.exp(d_last)[..., None] + jnp.einsum('bhid,bhie->bhde', k_ * jnp.exp(d_last - d)[..., None], v_new)
        return S, o

    xs = tuple(to_chunks(a.astype(jnp.float32), GD_CHUNK) for a in (q, k, v, beta, g))
    S0 = jnp.zeros((B, H, dk, dv), jnp.float32)
    _, o = lax.scan(step, S0, xs)
    return from_chunks(o)


def gdn_mixer(h, w_in, w_out, conv_w, a_log, dt_bias, onorm_g):
    B, T, _ = h.shape
    qkv, z, a, b = jnp.split(h @ w_in, [GD_QKV, GD_QKV + GD_VW, GD_QKV + GD_VW + GD_V_HEADS], axis=-1)
    qkv = jax.nn.silu(causal_conv(qkv, conv_w))
    q, k, v = jnp.split(qkv, [GD_QKW, 2 * GD_QKW], axis=-1)
    rep = GD_V_HEADS // GD_QK_HEADS
    q = jnp.repeat(l2norm(q.reshape(B, T, GD_QK_HEADS, GD_DK)) * (GD_DK ** -0.5), rep, axis=2)
    k = jnp.repeat(l2norm(k.reshape(B, T, GD_QK_HEADS, GD_DK)), rep, axis=2)
    v = v.reshape(B, T, GD_V_HEADS, GD_DV)
    beta = jax.nn.sigmoid(b.astype(jnp.float32))
    g = -jnp.exp(a_log.astype(jnp.float32)) * jax.nn.softplus(a.astype(jnp.float32) + dt_bias.astype(jnp.float32))
    o = gated_delta_scan(q, k, v, beta, g)
    o = rmsnorm(o, onorm_g).reshape(B, T, GD_VW).astype(h.dtype)
    return (o * jax.nn.silu(z)) @ w_out


def setup_inputs(seed: int = 0) -> dict:
    key = jax.random.key(seed)
    ks = jax.random.split(key, 24)
    nrm = lambda k, shape, s: jax.random.normal(k, shape, jnp.float32) * s
    x = nrm(ks[0], (BATCH, SEQ, D_MODEL), 1.0)
    c = nrm(ks[1], (BATCH, D_MODEL), 1.0)
    positions = (jnp.arange(SEQ, dtype=jnp.int32)[None, :]
                 + jax.random.randint(ks[2], (BATCH, 1), 0, POS_OFFSET_MAX, dtype=jnp.int32))
    hgrn_lb = nrm(ks[3], (DEPTH, D_MODEL), 0.1)
    ada_w = nrm(ks[4], (DEPTH, D_MODEL, 3 * D_MODEL), 0.5 * D_MODEL ** -0.5)
    ada_b = nrm(ks[5], (DEPTH, 3 * D_MODEL), 0.01)
    norm_g = 1.0 + nrm(ks[6], (DEPTH, D_MODEL), 0.02)
    hg_in_w = nrm(ks[7], (N_A, D_MODEL, HG_IN), D_MODEL ** -0.5)
    hg_out_w = nrm(ks[8], (N_A, HG_HEADS * HG_DV, D_MODEL), (HG_HEADS * HG_DV) ** -0.5)
    hg_onorm = 1.0 + nrm(ks[9], (N_A, HG_DV), 0.02)
    sw_in_w = nrm(ks[10], (N_B, D_MODEL, SW_IN), D_MODEL ** -0.5)
    sw_out_w = nrm(ks[11], (N_B, SW_QW, D_MODEL), SW_QW ** -0.5)
    sw_qnorm = 1.0 + nrm(ks[12], (N_B, SW_DH), 0.02)
    sw_knorm = 1.0 + nrm(ks[13], (N_B, SW_DH), 0.02)
    sw_sinks = nrm(ks[14], (N_B, SW_HEADS), 0.5)
    gd_in_w = nrm(ks[15], (N_C, D_MODEL, GD_IN), D_MODEL ** -0.5)
    gd_out_w = nrm(ks[16], (N_C, GD_VW, D_MODEL), GD_VW ** -0.5)
    gd_conv_w = nrm(ks[17], (N_C, GD_CONV, GD_QKV), GD_CONV ** -0.5)
    gd_a_log = jnp.log(jax.random.uniform(ks[18], (N_C, GD_V_HEADS), jnp.float32, 1.0, 16.0))
    dt = jnp.exp(jax.random.uniform(ks[19], (N_C, GD_V_HEADS), jnp.float32, math.log(1e-3), math.log(1e-1)))
    gd_dt_bias = dt + jnp.log(-jnp.expm1(-dt))
    gd_onorm = 1.0 + nrm(ks[20], (N_C, GD_DV), 0.02)
    return {'x': x, 'c': c, 'positions': positions, 'hgrn_lb': hgrn_lb,
            'ada_w': ada_w, 'ada_b': ada_b, 'norm_g': norm_g,
            'hg_in_w': hg_in_w, 'hg_out_w': hg_out_w, 'hg_onorm': hg_onorm,
            'sw_in_w': sw_in_w, 'sw_out_w': sw_out_w, 'sw_qnorm': sw_qnorm, 'sw_knorm': sw_knorm,
            'sw_sinks': sw_sinks,
            'gd_in_w': gd_in_w, 'gd_out_w': gd_out_w, 'gd_conv_w': gd_conv_w,
            'gd_a_log': gd_a_log, 'gd_dt_bias': gd_dt_bias, 'gd_onorm': gd_onorm}


def reference(x, c, positions, hgrn_lb, ada_w, ada_b, norm_g,
              hg_in_w, hg_out_w, hg_onorm,
              sw_in_w, sw_out_w, sw_qnorm, sw_knorm, sw_sinks,
              gd_in_w, gd_out_w, gd_conv_w, gd_a_log, gd_dt_bias, gd_onorm):
    lb_all = jnp.cumsum(jax.nn.softmax(hgrn_lb.astype(jnp.float32), axis=0), axis=0)
    lb_all = lb_all - lb_all[0:1]
    inv_freq = ROPE_THETA ** (-jnp.arange(0, SW_DH, 2, dtype=jnp.float32) / SW_DH)
    ang = positions.astype(jnp.float32)[..., None] * inv_freq
    for i in range(DEPTH):
        j = i // N_MIXERS
        mod = (c @ ada_w[i] + ada_b[i])[:, None, :]
        shift, scale, gate = jnp.split(mod, 3, axis=-1)
        h = rmsnorm(x, norm_g[i]) * (1.0 + scale) + shift
        kind = i % N_MIXERS
        if kind == 0:
            y = hgrn2_mixer(h, hg_in_w[j], hg_out_w[j], hg_onorm[j], lb_all[i])
        elif kind == 1:
            y = swa_mixer(h, sw_in_w[j], sw_out_w[j], sw_qnorm[j], sw_knorm[j], sw_sinks[j], ang)
        else:
            y = gdn_mixer(h, gd_in_w[j], gd_out_w[j], gd_conv_w[j], gd_a_log[j], gd_dt_bias[j], gd_onorm[j])
        x = x + gate * y
    return x
```

```python
import functools
import math

import numpy as np
import jax
import jax.numpy as jnp
from jax import lax
from jax.experimental import pallas as pl
from jax.experimental.pallas import tpu as pltpu

F32 = jnp.float32
BF16 = jnp.bfloat16

D_MODEL = 1024
DEPTH = 4
EPS = 1e-6
LANES = 128
SUBLANES = 8
VMEM_LIMIT = 56 * 2**20

HG_HEADS = 8
HG_DK = 128
HG_CHUNK = 64

SW_HEADS = 16
SW_KV_HEADS = 4
SW_DH = 64
SW_BLOCK = 128
SW_QW = SW_HEADS * SW_DH
SW_KVW = SW_KV_HEADS * SW_DH
ROPE_THETA = 10000.0

GD_QK_HEADS = 8
GD_V_HEADS = 16
GD_DK = 128
GD_DV = 128
GD_CONV = 4
GD_CHUNK = 64
GD_QKW = GD_QK_HEADS * GD_DK
GD_VW = GD_V_HEADS * GD_DV
GD_QKV = 2 * GD_QKW + GD_VW

ROW_TILE = 256


def _params(n_axes):
    return pltpu.CompilerParams(dimension_semantics=("arbitrary",) * n_axes,
                                vmem_limit_bytes=VMEM_LIMIT)


def _dot(a, b):
    return jnp.dot(a, b, preferred_element_type=F32)


def _dot_nt(a, b):
    return lax.dot_general(a, b, (((1,), (1,)), ((), ())), preferred_element_type=F32)


def _dot_tn(a, b):
    return lax.dot_general(a, b, (((0,), (0,)), ((), ())), preferred_element_type=F32)


def _split2(x):
    hi = x.astype(BF16)
    lo = (x - hi.astype(F32)).astype(BF16)
    return hi, lo


def _dot_exact01(m01, x):
    hi, lo = _split2(x)
    return _dot(m01, hi) + _dot(m01, lo)


def _sigmoid(x):
    return 1.0 / (1.0 + jnp.exp(-x))


def _silu(x):
    return x * _sigmoid(x)


def _ada_kernel(c_ref, w_ref, b_ref, o_ref):
    o_ref[...] = jnp.dot(c_ref[...], w_ref[...], precision=lax.Precision.HIGHEST,
                         preferred_element_type=F32) + b_ref[...]


def _ada_mod(c, ada_w, ada_b):
    B = c.shape[0]
    depth = ada_w.shape[0]
    c8 = jnp.zeros((SUBLANES, D_MODEL), F32).at[:B].set(c)
    return pl.pallas_call(
        _ada_kernel,
        grid=(depth, 3),
        in_specs=[pl.BlockSpec((SUBLANES, D_MODEL), lambda i, j: (0, 0)),
                  pl.BlockSpec((None, D_MODEL, D_MODEL), lambda i, j: (i, 0, j)),
                  pl.BlockSpec((None, 1, D_MODEL), lambda i, j: (i, 0, j))],
        out_specs=pl.BlockSpec((None, SUBLANES, D_MODEL), lambda i, j: (i, 0, j)),
        out_shape=jax.ShapeDtypeStruct((depth, SUBLANES, 3 * D_MODEL), F32),
        compiler_params=_params(2),
        name="ada_mod",
    )(c8, ada_w, ada_b.reshape(depth, 1, 3 * D_MODEL))


def _prenorm(x_ref, g_ref, mod_ref):
    b = pl.program_id(0)
    x = x_ref[...]
    y = x * lax.rsqrt(jnp.mean(x * x, axis=-1, keepdims=True) + EPS) * g_ref[...]
    m = mod_ref[pl.ds(b, 1), :]
    shift = m[:, :D_MODEL]
    scale = m[:, D_MODEL:2 * D_MODEL]
    return (y * (1.0 + scale) + shift).astype(BF16)


def _row_spec(tm, width, t_tiles):
    return pl.BlockSpec((tm, width), lambda b, i: (b * t_tiles + i, 0))


def _const_spec(shape):
    return pl.BlockSpec(shape, lambda b, i: (0,) * len(shape))


def _mod_spec(layer):
    return pl.BlockSpec((None, SUBLANES, 3 * D_MODEL), lambda b, i: (layer, 0, 0))


def _out_kernel(o_ref, z_ref, x_ref, mod_ref, w_ref, out_ref):
    b = pl.program_id(0)
    z = z_ref[...].astype(F32)
    y = (o_ref[...] * _silu(z)).astype(BF16)
    acc = _dot(y, w_ref[...])
    gate = mod_ref[pl.ds(b, 1), :][:, 2 * D_MODEL:]
    out_ref[...] = x_ref[...] + gate * acc


def _out_proj(o, z, x2, mod, layer, w_out, B, T):
    tm = ROW_TILE
    tt = T // tm
    wo = w_out.shape[0]
    return pl.pallas_call(
        _out_kernel,
        grid=(B, tt),
        in_specs=[_row_spec(tm, wo, tt), _row_spec(tm, wo, tt), _row_spec(tm, D_MODEL, tt),
                  _mod_spec(layer), _const_spec((wo, D_MODEL))],
        out_specs=_row_spec(tm, D_MODEL, tt),
        out_shape=jax.ShapeDtypeStruct((B * T, D_MODEL), F32),
        compiler_params=_params(2),
        name=f"out_proj_{layer}",
    )(o, z, x2, mod, w_out.astype(BF16))


def _hg_in_kernel(x_ref, g_ref, mod_ref, w_ref, lb_ref, q_ref, k_ref, v_ref, z_ref, lf_ref):
    h = _prenorm(x_ref, g_ref, mod_ref)
    lb = lb_ref[...]
    qp = _dot(h, w_ref[:, 0:D_MODEL])
    q_ref[...] = _silu(qp).astype(BF16)
    fp = _dot(h, w_ref[:, D_MODEL:2 * D_MODEL])
    sig = _sigmoid(fp)
    lf_ref[...] = jnp.log(lb + (1.0 - lb) * sig)
    k_ref[...] = ((1.0 - lb) * _sigmoid(-fp)).astype(BF16)
    v_ref[...] = _dot(h, w_ref[:, 2 * D_MODEL:3 * D_MODEL]).astype(BF16)
    z_ref[...] = _dot(h, w_ref[:, 3 * D_MODEL:4 * D_MODEL]).astype(BF16)


def _hg_levels(L):
    m, out = L // 2, []
    while m >= 1:
        out.append(m)
        m //= 2
    return out


def _hg_tables(L):
    levels = _hg_levels(L)
    t = np.arange(L)[:, None]
    u = np.arange(L)[None, :]
    secs = [(u <= t), (u > t)]
    masks = [np.eye(L, dtype=bool)]
    for m in levels:
        r = (t // (2 * m)) * (2 * m) + m
        upper = t >= r
        secs.append(np.where(upper, (u >= r) & (u <= t), (u > t) & (u < r)))
        ru = (u // (2 * m)) * (2 * m) + m
        masks.append((t // (2 * m) == u // (2 * m)) & (t >= r) & (u < ru))
    cum = np.concatenate(secs, axis=0).astype(np.float32)
    return jnp.asarray(cum, BF16), jnp.asarray(np.stack(masks).astype(np.float32))


def _hg_scan_kernel(q_ref, k_ref, v_ref, lf_ref, cum_ref, msk_ref, og_ref, o_ref, st_ref, *, L):
    levels = _hg_levels(L)

    @pl.when(pl.program_id(1) == 0)
    def _():
        st_ref[...] = jnp.zeros_like(st_ref)

    cum = cum_ref[...]
    row = lax.broadcasted_iota(jnp.int32, (L, HG_DK), 0)
    uppers = [(row & m) != 0 for m in levels]
    og = og_ref[...]
    for h in range(HG_HEADS):
        hs = slice(h * HG_DK, (h + 1) * HG_DK)
        ex = jnp.exp(_dot_exact01(cum, lf_ref[:, hs]))
        q = q_ref[:, hs].astype(F32)
        k = k_ref[:, hs].astype(F32)
        v = v_ref[:, hs]
        a = msk_ref[0] * _dot_nt(q_ref[:, hs], k_ref[:, hs])
        for li in range(len(levels)):
            e = ex[(2 + li) * L:(3 + li) * L]
            xm = (jnp.where(uppers[li], q, k) * e).astype(BF16)
            a = a + msk_ref[li + 1] * _dot_nt(xm, xm)
        st = st_ref[h]
        eb = ex[0:L]
        qd = (q * eb).astype(BF16)
        o = _dot(a.astype(BF16), v) + _dot_nt(qd, st.astype(BF16))
        kd = (k * ex[L:2 * L]).astype(BF16)
        st_ref[h] = st * eb[L - 1:L, :] + _dot_tn(v, kd)
        on = o * lax.rsqrt(jnp.mean(o * o, axis=-1, keepdims=True) + EPS) * og
        o_ref[:, hs] = on


def _hgrn2_layer(x2, mod, layer, norm_g, lb, w_in, w_out, onorm_g, B, T):
    tm = ROW_TILE
    tt = T // tm
    N = B * T
    q, k, v, z, lf = pl.pallas_call(
        _hg_in_kernel,
        grid=(B, tt),
        in_specs=[_row_spec(tm, D_MODEL, tt), _const_spec((1, D_MODEL)), _mod_spec(layer),
                  _const_spec((D_MODEL, 4 * D_MODEL)), _const_spec((1, D_MODEL))],
        out_specs=[_row_spec(tm, D_MODEL, tt)] * 5,
        out_shape=[jax.ShapeDtypeStruct((N, D_MODEL), BF16)] * 4 + [jax.ShapeDtypeStruct((N, D_MODEL), F32)],
        compiler_params=_params(2),
        name=f"hgrn2_in_{layer}",
    )(x2, norm_g.reshape(1, D_MODEL), mod, w_in.astype(BF16), lb.reshape(1, D_MODEL))

    L = HG_CHUNK
    tc = T // L
    cum, msk = _hg_tables(L)
    o = pl.pallas_call(
        functools.partial(_hg_scan_kernel, L=L),
        grid=(B, tc),
        in_specs=[_row_spec(L, D_MODEL, tc)] * 4 + [_const_spec(cum.shape), _const_spec(msk.shape),
                                                     _const_spec((1, HG_DK))],
        out_specs=_row_spec(L, D_MODEL, tc),
        out_shape=jax.ShapeDtypeStruct((N, D_MODEL), F32),
        scratch_shapes=[pltpu.VMEM((HG_HEADS, HG_DK, HG_DK), F32)],
        compiler_params=_params(2),
        name=f"hgrn2_scan_{layer}",
    )(q, k, v, lf, cum, msk, onorm_g.reshape(1, HG_DK))
    return _out_proj(o, z, x2, mod, layer, w_out, B, T)


def _sw_in_kernel(x_ref, g_ref, mod_ref, w_ref, pos_ref, invf_ref, qg_ref, kg_ref, seg_ref,
                  q_ref, kd_ref, vd_ref, z_ref):
    h = _prenorm(x_ref, g_ref, mod_ref)
    tm = h.shape[0]
    ang = pos_ref[...].astype(F32) * invf_ref[...]
    cos_t = jnp.cos(ang)
    sin_t = jnp.sin(ang)
    lane = lax.broadcasted_iota(jnp.int32, (tm, LANES), 1)
    first = (lane & (SW_DH // 2)) == 0
    sin_a = jnp.where(first, -sin_t, 0.0)
    sin_b = jnp.where(first, 0.0, sin_t)
    low = lane < SW_DH
    seg = seg_ref[...]

    def norm_rope(xg, gain):
        ms = _dot((xg * xg).astype(BF16), seg)
        xn = xg * lax.rsqrt(ms + EPS) * gain
        return (xn * cos_t + pltpu.roll(xn, LANES - SW_DH // 2, 1) * sin_a
                + pltpu.roll(xn, SW_DH // 2, 1) * sin_b)

    scale = SW_DH ** -0.5
    for c in range(SW_QW // LANES):
        xg = _dot(h, w_ref[:, c * LANES:(c + 1) * LANES])
        q_ref[:, c * LANES:(c + 1) * LANES] = (norm_rope(xg, qg_ref[...]) * scale).astype(BF16)
    for c in range(SW_KVW // LANES):
        xg = _dot(h, w_ref[:, SW_QW + c * LANES:SW_QW + (c + 1) * LANES])
        kr = norm_rope(xg, kg_ref[...])
        sw = pltpu.roll(kr, SW_DH, 1)
        for hh, (in_low, in_high) in enumerate(((kr, sw), (sw, kr))):
            base = (2 * c + hh) * 2 * LANES
            kd_ref[:, base:base + LANES] = jnp.where(low, in_low, 0.0).astype(BF16)
            kd_ref[:, base + LANES:base + 2 * LANES] = jnp.where(low, 0.0, in_high).astype(BF16)
        vg = _dot(h, w_ref[:, SW_QW + SW_KVW + c * LANES:SW_QW + SW_KVW + (c + 1) * LANES])
        sv = pltpu.roll(vg, SW_DH, 1)
        vd_ref[:, (2 * c) * LANES:(2 * c + 1) * LANES] = jnp.where(low, vg, sv).astype(BF16)
        vd_ref[:, (2 * c + 1) * LANES:(2 * c + 2) * LANES] = jnp.where(low, sv, vg).astype(BF16)
    z_ref[...] = _dot(h, w_ref[:, SW_QW + 2 * SW_KVW:]).astype(BF16)


def _sw_attn_kernel(sink_ref, q_ref, kc_ref, kp_ref, vc_ref, vp_ref, o_ref):
    n = pl.program_id(1)
    BLK = SW_BLOCK
    qi = lax.broadcasted_iota(jnp.int32, (BLK, 2 * BLK), 0)
    kj = lax.broadcasted_iota(jnp.int32, (BLK, 2 * BLK), 1)
    allowed = ((kj < BLK) & (kj > qi) & (n > 0)) | ((kj >= BLK) & ((kj - BLK) <= qi))
    bias = jnp.where(allowed, 0.0, -jnp.inf)
    lane = lax.broadcasted_iota(jnp.int32, (BLK, LANES), 1)
    low = lane < SW_DH
    group = SW_HEADS // SW_KV_HEADS
    for h in range(SW_KV_HEADS):
        hs = slice(h * LANES, (h + 1) * LANES)
        vv = jnp.concatenate([vp_ref[:, hs], vc_ref[:, hs]], axis=0)
        kks = []
        for half in range(2):
            ks = slice((2 * h + half) * LANES, (2 * h + half + 1) * LANES)
            kks.append(jnp.concatenate([kp_ref[:, ks], kc_ref[:, ks]], axis=0))
        for j in range(group // 2):
            c = h * (group // 2) + j
            qp = q_ref[:, c * LANES:(c + 1) * LANES]
            outs = []
            for half in range(2):
                s = _dot_nt(qp, kks[half]) + bias
                sink = sink_ref[2 * c + half]
                m = jnp.maximum(jnp.max(s, axis=-1, keepdims=True), sink)
                p = jnp.exp(s - m)
                l = jnp.sum(p, axis=-1, keepdims=True) + jnp.exp(sink - m)
                outs.append(_dot(p.astype(BF16), vv) / l)
            o_ref[:, c * LANES:(c + 1) * LANES] = jnp.where(low, outs[0], outs[1])


def _swa_layer(x2, mod, layer, norm_g, positions, w_in, w_out, qn_g, kn_g, sinks, B, T):
    tm = ROW_TILE
    tt = T // tm
    N = B * T
    inv_freq = ROPE_THETA ** (-jnp.arange(0, SW_DH, 2, dtype=F32) / SW_DH)
    invf = jnp.tile(inv_freq, LANES // (SW_DH // 2)).reshape(1, LANES)
    seg = jnp.asarray(np.kron(np.eye(LANES // SW_DH), np.full((SW_DH, SW_DH), 1.0 / SW_DH)), BF16)
    tile2 = lambda g: jnp.tile(g.astype(F32), LANES // SW_DH).reshape(1, LANES)
    q, kd, vd, z = pl.pallas_call(
        _sw_in_kernel,
        grid=(B, tt),
        in_specs=[_row_spec(tm, D_MODEL, tt), _const_spec((1, D_MODEL)), _mod_spec(layer),
                  _const_spec((D_MODEL, w_in.shape[1])), _row_spec(tm, 1, tt),
                  _const_spec((1, LANES)), _const_spec((1, LANES)), _const_spec((1, LANES)),
                  _const_spec((LANES, LANES))],
        out_specs=[_row_spec(tm, SW_QW, tt), _row_spec(tm, 4 * SW_KVW, tt),
                   _row_spec(tm, 2 * SW_KVW, tt), _row_spec(tm, SW_QW, tt)],
        out_shape=[jax.ShapeDtypeStruct((N, SW_QW), BF16), jax.ShapeDtypeStruct((N, 4 * SW_KVW), BF16),
                   jax.ShapeDtypeStruct((N, 2 * SW_KVW), BF16), jax.ShapeDtypeStruct((N, SW_QW), BF16)],
        compiler_params=_params(2),
        name=f"swa_in_{layer}",
    )(x2, norm_g.reshape(1, D_MODEL), mod, w_in.astype(BF16), positions.reshape(N, 1),
      invf, tile2(qn_g), tile2(kn_g), seg)

    nb = T // SW_BLOCK
    cur = lambda b, n: (b * nb + n, 0)
    prev = lambda b, n: (b * nb + jnp.maximum(n - 1, 0), 0)
    kw, vw = 4 * SW_KVW, 2 * SW_KVW
    o = pl.pallas_call(
        _sw_attn_kernel,
        grid=(B, nb),
        in_specs=[pl.BlockSpec(memory_space=pltpu.SMEM),
                  pl.BlockSpec((SW_BLOCK, SW_QW), cur),
                  pl.BlockSpec((SW_BLOCK, kw), cur), pl.BlockSpec((SW_BLOCK, kw), prev),
                  pl.BlockSpec((SW_BLOCK, vw), cur), pl.BlockSpec((SW_BLOCK, vw), prev)],
        out_specs=pl.BlockSpec((SW_BLOCK, SW_QW), cur),
        out_shape=jax.ShapeDtypeStruct((N, SW_QW), F32),
        compiler_params=_params(2),
        name=f"swa_attn_{layer}",
    )(sinks.astype(F32), q, kd, kd, vd, vd)
    return _out_proj(o, z, x2, mod, layer, w_out, B, T)


def _gd_in_kernel(x_ref, g_ref, mod_ref, w_ref, wab_hi_ref, wab_lo_ref, cw_ref, alog_ref, dtb_ref,
                  q_ref, k_ref, v_ref, z_ref, gb_ref, hist_ref):
    tm = x_ref.shape[0]
    H = SUBLANES

    @pl.when(pl.program_id(1) == 0)
    def _():
        hist_ref[0:H, :] = jnp.zeros((H, GD_QKV), F32)

    x = x_ref[...]
    b = pl.program_id(0)
    y = x * lax.rsqrt(jnp.mean(x * x, axis=-1, keepdims=True) + EPS) * g_ref[...]
    m = mod_ref[pl.ds(b, 1), :]
    hf = y * (1.0 + m[:, D_MODEL:2 * D_MODEL]) + m[:, :D_MODEL]
    h = hf.astype(BF16)
    h_lo = (hf - h.astype(F32)).astype(BF16)

    for s in range(GD_QKV // D_MODEL):
        cs = slice(s * D_MODEL, (s + 1) * D_MODEL)
        hist_ref[H:H + tm, cs] = _dot(h, w_ref[:, cs])
        conv = cw_ref[GD_CONV - 1:GD_CONV, cs] * hist_ref[H:H + tm, cs]
        for j in range(1, GD_CONV):
            conv = conv + cw_ref[GD_CONV - 1 - j:GD_CONV - j, cs] * hist_ref[H - j:H - j + tm, cs]
        act = _silu(conv)
        if s < 2:
            for c in range(D_MODEL // GD_DK):
                a = act[:, c * GD_DK:(c + 1) * GD_DK]
                r = lax.rsqrt(jnp.sum(a * a, axis=-1, keepdims=True) + EPS)
                if s == 0:
                    q_ref[:, c * GD_DK:(c + 1) * GD_DK] = (a * r * GD_DK ** -0.5).astype(BF16)
                else:
                    k_ref[:, c * GD_DK:(c + 1) * GD_DK] = (a * r).astype(BF16)
        else:
            v_ref[:, (s - 2) * D_MODEL:(s - 1) * D_MODEL] = act.astype(BF16)
    hist_ref[0:H, :] = hist_ref[tm:tm + H, :]

    z_ref[...] = _dot(h, w_ref[:, GD_QKV:GD_QKV + GD_VW]).astype(BF16)

    ab = _dot(h, wab_hi_ref[...]) + _dot(h, wab_lo_ref[...]) + _dot(h_lo, wab_hi_ref[...])
    xa = ab + dtb_ref[...]
    softplus = jnp.maximum(xa, 0.0) + jnp.log(1.0 + jnp.exp(-jnp.abs(xa)))
    gv = -jnp.exp(alog_ref[...]) * softplus
    lane = lax.broadcasted_iota(jnp.int32, (tm, LANES), 1)
    gb_ref[...] = jnp.where(lane < GD_V_HEADS, gv, _sigmoid(ab))


def _stack_lhs(x):
    hi = x.astype(BF16)
    hi32 = hi.astype(F32)
    c0 = (hi32 + pltpu.roll(x - hi32, GD_CHUNK, 1)).astype(BF16)
    return jnp.concatenate([c0, hi], axis=1)


def _stack_rhs(x):
    hi, lo = _split2(x)
    return jnp.concatenate([hi, hi, lo, jnp.zeros_like(hi)], axis=0)


def _gd_scan_kernel(q_ref, k_ref, v_ref, gb_ref, tri_ref, og_ref, o_ref, st_ref):
    C = GD_CHUNK

    @pl.when(pl.program_id(1) == 0)
    def _():
        st_ref[...] = jnp.zeros_like(st_ref)

    ti = lax.broadcasted_iota(jnp.int32, (C, LANES), 0)
    tj = lax.broadcasted_iota(jnp.int32, (C, LANES), 1)
    incl = tj <= ti
    strict = tj < ti
    eye = (ti == tj).astype(F32)
    gb = gb_ref[...]
    d_all = _dot_exact01(tri_ref[...], gb)
    og = og_ref[...]
    zrow = jnp.zeros((LANES - C, GD_DK), BF16)
    rep = GD_V_HEADS // GD_QK_HEADS
    for hq in range(GD_QK_HEADS):
        qs = slice(hq * GD_DK, (hq + 1) * GD_DK)
        qb = q_ref[:, qs]
        kb16 = k_ref[:, qs]
        q = qb.astype(F32)
        k = kb16.astype(F32)
        kpad = jnp.concatenate([kb16, zrow], axis=0)
        kk = _dot_nt(kb16, kpad)
        qk = _dot_nt(qb, kpad)
        for r in range(rep):
            hv = hq * rep + r
            g_col = gb[:, hv:hv + 1]
            beta = gb[:, GD_V_HEADS + hv:GD_V_HEADS + hv + 1]
            d_col = d_all[:, hv:hv + 1]
            d_row = jnp.sum(jnp.where(ti <= tj, g_col, 0.0), axis=0, keepdims=True)
            dec = jnp.exp(jnp.where(incl, d_col - d_row, -jnp.inf))
            a = kk * beta * jnp.where(strict, dec, 0.0)
            t_inv = eye - a
            p = a
            for _ in range(int(math.log2(C)) - 1):
                p = _dot(_stack_lhs(p), _stack_rhs(p))
                t_inv = t_inv + _dot(_stack_lhs(t_inv), _stack_rhs(p))
            ed = jnp.exp(d_col)
            v = v_ref[:, hv * GD_DV:(hv + 1) * GD_DV].astype(F32)
            rhs = jnp.concatenate([v * beta, k * beta * ed], axis=1).astype(BF16)
            rhs = jnp.concatenate([rhs, jnp.zeros_like(rhs)], axis=0)
            uw = _dot(t_inv.astype(BF16), rhs)
            u = uw[:, :GD_DV]
            w = uw[:, GD_DV:]
            s_prev = st_ref[hv]
            s16 = s_prev.astype(BF16)
            v_new = u - _dot(w.astype(BF16), s16)
            vn16 = v_new.astype(BF16)
            o = (_dot((q * ed).astype(BF16), s16)
                 + _dot((qk * dec).astype(BF16), jnp.concatenate([vn16, zrow], axis=0)))
            d_last = d_col[C - 1:C, :]
            kd = (k * jnp.exp(d_last - d_col)).astype(BF16)
            st_ref[hv] = s_prev * jnp.exp(d_last) + _dot_tn(kd, vn16)
            on = o * lax.rsqrt(jnp.mean(o * o, axis=-1, keepdims=True) + EPS) * og
            o_ref[:, hv * GD_DV:(hv + 1) * GD_DV] = on


def _gdn_layer(x2, mod, layer, norm_g, w_in, w_out, conv_w, a_log, dt_bias, onorm_g, B, T):
    tm = ROW_TILE
    tt = T // tm
    N = B * T
    w_main = w_in[:, :GD_QKV + GD_VW].astype(BF16)
    w_ab = jnp.zeros((D_MODEL, LANES), F32).at[:, :2 * GD_V_HEADS].set(w_in[:, GD_QKV + GD_VW:])
    wab_hi = w_ab.astype(BF16)
    wab_lo = (w_ab - wab_hi.astype(F32)).astype(BF16)
    pad = lambda a: jnp.zeros((1, LANES), F32).at[0, :GD_V_HEADS].set(a.astype(F32))
    q, k, v, z, gb = pl.pallas_call(
        _gd_in_kernel,
        grid=(B, tt),
        in_specs=[_row_spec(tm, D_MODEL, tt), _const_spec((1, D_MODEL)), _mod_spec(layer),
                  _const_spec((D_MODEL, GD_QKV + GD_VW)), _const_spec((D_MODEL, LANES)),
                  _const_spec((D_MODEL, LANES)), _const_spec((GD_CONV, GD_QKV)),
                  _const_spec((1, LANES)), _const_spec((1, LANES))],
        out_specs=[_row_spec(tm, GD_QKW, tt), _row_spec(tm, GD_QKW, tt), _row_spec(tm, GD_VW, tt),
                   _row_spec(tm, GD_VW, tt), _row_spec(tm, LANES, tt)],
        out_shape=[jax.ShapeDtypeStruct((N, GD_QKW), BF16), jax.ShapeDtypeStruct((N, GD_QKW), BF16),
                   jax.ShapeDtypeStruct((N, GD_VW), BF16), jax.ShapeDtypeStruct((N, GD_VW), BF16),
                   jax.ShapeDtypeStruct((N, LANES), F32)],
        scratch_shapes=[pltpu.VMEM((tm + 2 * SUBLANES, GD_QKV), F32)],
        compiler_params=_params(2),
        name=f"gdn_in_{layer}",
    )(x2, norm_g.reshape(1, D_MODEL), mod, w_main, wab_hi, wab_lo, conv_w.astype(F32),
      pad(a_log), pad(dt_bias))

    C = GD_CHUNK
    tc = T // C
    tri = jnp.asarray(np.tril(np.ones((C, C), np.float32)), BF16)
    o = pl.pallas_call(
        _gd_scan_kernel,
        grid=(B, tc),
        in_specs=[_row_spec(C, GD_QKW, tc), _row_spec(C, GD_QKW, tc), _row_spec(C, GD_VW, tc),
                  _row_spec(C, LANES, tc), _const_spec((C, C)), _const_spec((1, GD_DV))],
        out_specs=_row_spec(C, GD_VW, tc),
        out_shape=jax.ShapeDtypeStruct((N, GD_VW), F32),
        scratch_shapes=[pltpu.VMEM((GD_V_HEADS, GD_DK, GD_DV), F32)],
        compiler_params=_params(2),
        name=f"gdn_scan_{layer}",
    )(q, k, v, gb, tri, onorm_g.reshape(1, GD_DV))
    return _out_proj(o, z, x2, mod, layer, w_out, B, T)


def kernel(x, c, positions, hgrn_lb, ada_w, ada_b, norm_g, hg_in_w, hg_out_w, hg_onorm, sw_in_w, sw_out_w,
           sw_qnorm, sw_knorm, sw_sinks, gd_in_w, gd_out_w, gd_conv_w, gd_a_log, gd_dt_bias, gd_onorm):
    B, T, _ = x.shape
    lb_all = jnp.cumsum(jax.nn.softmax(hgrn_lb.astype(F32), axis=0), axis=0)
    lb_all = lb_all - lb_all[0:1]
    mod = _ada_mod(c, ada_w, ada_b)
    x2 = x.reshape(B * T, D_MODEL)
    for i in range(DEPTH):
        j, kind = divmod(i, 3)
        if kind == 0:
            x2 = _hgrn2_layer(x2, mod, i, norm_g[i], lb_all[i], hg_in_w[j], hg_out_w[j], hg_onorm[j], B, T)
        elif kind == 1:
            x2 = _swa_layer(x2, mod, i, norm_g[i], positions, sw_in_w[j], sw_out_w[j], sw_qnorm[j],
                            sw_knorm[j], sw_sinks[j], B, T)
        else:
            x2 = _gdn_layer(x2, mod, i, norm_g[i], gd_in_w[j], gd_out_w[j], gd_conv_w[j], gd_a_log[j],
                            gd_dt_bias[j], gd_onorm[j], B, T)
    return x2.reshape(B, T, D_MODEL)
```

```python
import functools
import math

import numpy as np
import jax
import jax.numpy as jnp
from jax import lax
from jax.experimental import pallas as pl
from jax.experimental.pallas import tpu as pltpu

F32 = jnp.float32
BF16 = jnp.bfloat16

D_MODEL = 1024
DEPTH = 4
EPS = 1e-6
LANES = 128
SUBLANES = 8
VMEM_LIMIT = 56 * 2**20

HG_HEADS = 8
HG_DK = 128
HG_CHUNK = 64

SW_HEADS = 16
SW_KV_HEADS = 4
SW_DH = 64
SW_BLOCK = 128
SW_QW = SW_HEADS * SW_DH
SW_KVW = SW_KV_HEADS * SW_DH
ROPE_THETA = 10000.0

GD_QK_HEADS = 8
GD_V_HEADS = 16
GD_DK = 128
GD_DV = 128
GD_CONV = 4
GD_CHUNK = 64
GD_QKW = GD_QK_HEADS * GD_DK
GD_VW = GD_V_HEADS * GD_DV
GD_QKV = 2 * GD_QKW + GD_VW

ROW_TILE = 256


def _params(n_axes):
    return pltpu.CompilerParams(dimension_semantics=("arbitrary",) * n_axes,
                                vmem_limit_bytes=VMEM_LIMIT)


def _dot(a, b):
    return jnp.dot(a, b, preferred_element_type=F32)


def _dot_nt(a, b):
    return lax.dot_general(a, b, (((1,), (1,)), ((), ())), preferred_element_type=F32)


def _dot_tn(a, b):
    return lax.dot_general(a, b, (((0,), (0,)), ((), ())), preferred_element_type=F32)


def _split2(x):
    hi = x.astype(BF16)
    lo = (x - hi.astype(F32)).astype(BF16)
    return hi, lo


def _dot_exact01(m01, x):
    hi, lo = _split2(x)
    return _dot(m01, hi) + _dot(m01, lo)


def _sigmoid(x):
    return 1.0 / (1.0 + jnp.exp(-x))


def _silu(x):
    return x * _sigmoid(x)


def _ada_kernel(c_ref, w_ref, b_ref, o_ref):
    o_ref[...] = jnp.dot(c_ref[...], w_ref[...], precision=lax.Precision.HIGHEST,
                         preferred_element_type=F32) + b_ref[...]


def _ada_mod(c, ada_w, ada_b):
    B = c.shape[0]
    depth = ada_w.shape[0]
    c8 = jnp.zeros((SUBLANES, D_MODEL), F32).at[:B].set(c)
    return pl.pallas_call(
        _ada_kernel,
        grid=(depth, 3),
        in_specs=[pl.BlockSpec((SUBLANES, D_MODEL), lambda i, j: (0, 0)),
                  pl.BlockSpec((None, D_MODEL, D_MODEL), lambda i, j: (i, 0, j)),
                  pl.BlockSpec((None, 1, D_MODEL), lambda i, j: (i, 0, j))],
        out_specs=pl.BlockSpec((None, SUBLANES, D_MODEL), lambda i, j: (i, 0, j)),
        out_shape=jax.ShapeDtypeStruct((depth, SUBLANES, 3 * D_MODEL), F32),
        compiler_params=_params(2),
        name="ada_mod",
    )(c8, ada_w, ada_b.reshape(depth, 1, 3 * D_MODEL))


def _prenorm(x_ref, g_ref, mod_ref):
    b = pl.program_id(0)
    x = x_ref[...]
    y = x * lax.rsqrt(jnp.mean(x * x, axis=-1, keepdims=True) + EPS) * g_ref[...]
    m = mod_ref[pl.ds(b, 1), :]
    shift = m[:, :D_MODEL]
    scale = m[:, D_MODEL:2 * D_MODEL]
    return (y * (1.0 + scale) + shift).astype(BF16)


def _row_spec(tm, width, t_tiles):
    return pl.BlockSpec((tm, width), lambda b, i: (b * t_tiles + i, 0))


def _const_spec(shape):
    return pl.BlockSpec(shape, lambda b, i: (0,) * len(shape))


def _mod_spec(layer):
    return pl.BlockSpec((None, SUBLANES, 3 * D_MODEL), lambda b, i: (layer, 0, 0))


def _out_kernel(o_ref, z_ref, x_ref, mod_ref, w_ref, out_ref):
    b = pl.program_id(0)
    z = z_ref[...].astype(F32)
    y = (o_ref[...] * _silu(z)).astype(BF16)
    acc = _dot(y, w_ref[...])
    gate = mod_ref[pl.ds(b, 1), :][:, 2 * D_MODEL:]
    out_ref[...] = x_ref[...] + gate * acc


def _out_proj(o, z, x2, mod, layer, w_out, B, T):
    tm = ROW_TILE
    tt = T // tm
    wo = w_out.shape[0]
    return pl.pallas_call(
        _out_kernel,
        grid=(B, tt),
        in_specs=[_row_spec(tm, wo, tt), _row_spec(tm, wo, tt), _row_spec(tm, D_MODEL, tt),
                  _mod_spec(layer), _const_spec((wo, D_MODEL))],
        out_specs=_row_spec(tm, D_MODEL, tt),
        out_shape=jax.ShapeDtypeStruct((B * T, D_MODEL), F32),
        compiler_params=_params(2),
        name=f"out_proj_{layer}",
    )(o, z, x2, mod, w_out.astype(BF16))


def _hg_in_kernel(x_ref, g_ref, mod_ref, w_ref, lb_ref, q_ref, k_ref, v_ref, z_ref, lf_ref):
    h = _prenorm(x_ref, g_ref, mod_ref)
    lb = lb_ref[...]
    qp = _dot(h, w_ref[:, 0:D_MODEL])
    q_ref[...] = _silu(qp).astype(BF16)
    fp = _dot(h, w_ref[:, D_MODEL:2 * D_MODEL])
    sig = _sigmoid(fp)
    lf_ref[...] = jnp.log(lb + (1.0 - lb) * sig)
    k_ref[...] = ((1.0 - lb) * _sigmoid(-fp)).astype(BF16)
    v_ref[...] = _dot(h, w_ref[:, 2 * D_MODEL:3 * D_MODEL]).astype(BF16)
    z_ref[...] = _dot(h, w_ref[:, 3 * D_MODEL:4 * D_MODEL]).astype(BF16)


def _hg_levels(L):
    m, out = L // 2, []
    while m >= 1:
        out.append(m)
        m //= 2
    return out


def _hg_tables(L):
    levels = _hg_levels(L)
    t = np.arange(L)[:, None]
    u = np.arange(L)[None, :]
    secs = [(u <= t), (u > t)]
    masks = [np.eye(L, dtype=bool)]
    for m in levels:
        r = (t // (2 * m)) * (2 * m) + m
        upper = t >= r
        secs.append(np.where(upper, (u >= r) & (u <= t), (u > t) & (u < r)))
        ru = (u // (2 * m)) * (2 * m) + m
        masks.append((t // (2 * m) == u // (2 * m)) & (t >= r) & (u < ru))
    cum = np.concatenate(secs, axis=0).astype(np.float32)
    return jnp.asarray(cum, BF16), jnp.asarray(np.stack(masks).astype(np.float32))


def _hg_scan_kernel(q_ref, k_ref, v_ref, lf_ref, cum_ref, msk_ref, og_ref, o_ref, st_ref, ex_ref, *, L):
    levels = _hg_levels(L)

    @pl.when(pl.program_id(1) == 0)
    def _():
        st_ref[...] = jnp.zeros_like(st_ref)

    row = lax.broadcasted_iota(jnp.int32, (L, HG_DK), 0)
    og = og_ref[...]
    heads = range(HG_HEADS)
    hs = [slice(h * HG_DK, (h + 1) * HG_DK) for h in heads]
    ex_ref[...] = jnp.exp(_dot_exact01(cum_ref[...], lf_ref[...]))
    a = [msk_ref[0] * _dot_nt(q_ref[:, hs[h]], k_ref[:, hs[h]]) for h in heads]
    for li, m in enumerate(levels):
        upper = (row & m) != 0
        for h in heads:
            e = ex_ref[(2 + li) * L:(3 + li) * L, hs[h]]
            x = jnp.where(upper, q_ref[:, hs[h]].astype(F32), k_ref[:, hs[h]].astype(F32))
            xm = (x * e).astype(BF16)
            a[h] = a[h] + msk_ref[li + 1] * _dot_nt(xm, xm)
    for h in heads:
        qd = (q_ref[:, hs[h]].astype(F32) * ex_ref[0:L, hs[h]]).astype(BF16)
        o = _dot(a[h].astype(BF16), v_ref[:, hs[h]]) + _dot_nt(qd, st_ref[h].astype(BF16))
        on = o * lax.rsqrt(jnp.mean(o * o, axis=-1, keepdims=True) + EPS) * og
        o_ref[:, hs[h]] = on
    for h in heads:
        kd = (k_ref[:, hs[h]].astype(F32) * ex_ref[L:2 * L, hs[h]]).astype(BF16)
        st_ref[h] = st_ref[h] * ex_ref[L - 1:L, hs[h]] + _dot_tn(v_ref[:, hs[h]], kd)


def _hgrn2_layer(x2, mod, layer, norm_g, lb, w_in, w_out, onorm_g, B, T):
    tm = ROW_TILE
    tt = T // tm
    N = B * T
    q, k, v, z, lf = pl.pallas_call(
        _hg_in_kernel,
        grid=(B, tt),
        in_specs=[_row_spec(tm, D_MODEL, tt), _const_spec((1, D_MODEL)), _mod_spec(layer),
                  _const_spec((D_MODEL, 4 * D_MODEL)), _const_spec((1, D_MODEL))],
        out_specs=[_row_spec(tm, D_MODEL, tt)] * 5,
        out_shape=[jax.ShapeDtypeStruct((N, D_MODEL), BF16)] * 4 + [jax.ShapeDtypeStruct((N, D_MODEL), F32)],
        compiler_params=_params(2),
        name=f"hgrn2_in_{layer}",
    )(x2, norm_g.reshape(1, D_MODEL), mod, w_in.astype(BF16), lb.reshape(1, D_MODEL))

    L = HG_CHUNK
    tc = T // L
    cum, msk = _hg_tables(L)
    o = pl.pallas_call(
        functools.partial(_hg_scan_kernel, L=L),
        grid=(B, tc),
        in_specs=[_row_spec(L, D_MODEL, tc)] * 4 + [_const_spec(cum.shape), _const_spec(msk.shape),
                                                     _const_spec((1, HG_DK))],
        out_specs=_row_spec(L, D_MODEL, tc),
        out_shape=jax.ShapeDtypeStruct((N, D_MODEL), F32),
        scratch_shapes=[pltpu.VMEM((HG_HEADS, HG_DK, HG_DK), F32),
                        pltpu.VMEM((cum.shape[0], D_MODEL), F32)],
        compiler_params=_params(2),
        name=f"hgrn2_scan_{layer}",
    )(q, k, v, lf, cum, msk, onorm_g.reshape(1, HG_DK))
    return _out_proj(o, z, x2, mod, layer, w_out, B, T)


def _sw_in_kernel(x_ref, g_ref, mod_ref, w_ref, pos_ref, invf_ref, qg_ref, kg_ref, seg_ref,
                  q_ref, kd_ref, vd_ref, z_ref):
    h = _prenorm(x_ref, g_ref, mod_ref)
    tm = h.shape[0]
    ang = pos_ref[...].astype(F32) * invf_ref[...]
    cos_t = jnp.cos(ang)
    sin_t = jnp.sin(ang)
    lane = lax.broadcasted_iota(jnp.int32, (tm, LANES), 1)
    first = (lane & (SW_DH // 2)) == 0
    sin_a = jnp.where(first, -sin_t, 0.0)
    sin_b = jnp.where(first, 0.0, sin_t)
    low = lane < SW_DH
    seg = seg_ref[...]

    def norm_rope(xg, gain):
        ms = _dot((xg * xg).astype(BF16), seg)
        xn = xg * lax.rsqrt(ms + EPS) * gain
        return (xn * cos_t + pltpu.roll(xn, LANES - SW_DH // 2, 1) * sin_a
                + pltpu.roll(xn, SW_DH // 2, 1) * sin_b)

    scale = SW_DH ** -0.5
    for c in range(SW_QW // LANES):
        xg = _dot(h, w_ref[:, c * LANES:(c + 1) * LANES])
        q_ref[:, c * LANES:(c + 1) * LANES] = (norm_rope(xg, qg_ref[...]) * scale).astype(BF16)
    for c in range(SW_KVW // LANES):
        xg = _dot(h, w_ref[:, SW_QW + c * LANES:SW_QW + (c + 1) * LANES])
        kr = norm_rope(xg, kg_ref[...])
        sw = pltpu.roll(kr, SW_DH, 1)
        for hh, (in_low, in_high) in enumerate(((kr, sw), (sw, kr))):
            base = (2 * c + hh) * 2 * LANES
            kd_ref[:, base:base + LANES] = jnp.where(low, in_low, 0.0).astype(BF16)
            kd_ref[:, base + LANES:base + 2 * LANES] = jnp.where(low, 0.0, in_high).astype(BF16)
        vg = _dot(h, w_ref[:, SW_QW + SW_KVW + c * LANES:SW_QW + SW_KVW + (c + 1) * LANES])
        sv = pltpu.roll(vg, SW_DH, 1)
        vd_ref[:, (2 * c) * LANES:(2 * c + 1) * LANES] = jnp.where(low, vg, sv).astype(BF16)
        vd_ref[:, (2 * c + 1) * LANES:(2 * c + 2) * LANES] = jnp.where(low, sv, vg).astype(BF16)
    z_ref[...] = _dot(h, w_ref[:, SW_QW + 2 * SW_KVW:]).astype(BF16)


def _sw_attn_kernel(sink_ref, q_ref, kc_ref, kp_ref, vc_ref, vp_ref, o_ref):
    n = pl.program_id(1)
    BLK = SW_BLOCK
    qi = lax.broadcasted_iota(jnp.int32, (BLK, 2 * BLK), 0)
    kj = lax.broadcasted_iota(jnp.int32, (BLK, 2 * BLK), 1)
    allowed = ((kj < BLK) & (kj > qi) & (n > 0)) | ((kj >= BLK) & ((kj - BLK) <= qi))
    bias = jnp.where(allowed, 0.0, -jnp.inf)
    lane = lax.broadcasted_iota(jnp.int32, (BLK, LANES), 1)
    low = lane < SW_DH
    group = SW_HEADS // SW_KV_HEADS
    for h in range(SW_KV_HEADS):
        hs = slice(h * LANES, (h + 1) * LANES)
        vv = jnp.concatenate([vp_ref[:, hs], vc_ref[:, hs]], axis=0)
        kks = []
        for half in range(2):
            ks = slice((2 * h + half) * LANES, (2 * h + half + 1) * LANES)
            kks.append(jnp.concatenate([kp_ref[:, ks], kc_ref[:, ks]], axis=0))
        for j in range(group // 2):
            c = h * (group // 2) + j
            qp = q_ref[:, c * LANES:(c + 1) * LANES]
            outs = []
            for half in range(2):
                s = _dot_nt(qp, kks[half]) + bias
                sink = sink_ref[2 * c + half]
                m = jnp.maximum(jnp.max(s, axis=-1, keepdims=True), sink)
                p = jnp.exp(s - m)
                l = jnp.sum(p, axis=-1, keepdims=True) + jnp.exp(sink - m)
                outs.append(_dot(p.astype(BF16), vv) / l)
            o_ref[:, c * LANES:(c + 1) * LANES] = jnp.where(low, outs[0], outs[1])


def _swa_layer(x2, mod, layer, norm_g, positions, w_in, w_out, qn_g, kn_g, sinks, B, T):
    tm = ROW_TILE
    tt = T // tm
    N = B * T
    inv_freq = ROPE_THETA ** (-jnp.arange(0, SW_DH, 2, dtype=F32) / SW_DH)
    invf = jnp.tile(inv_freq, LANES // (SW_DH // 2)).reshape(1, LANES)
    seg = jnp.asarray(np.kron(np.eye(LANES // SW_DH), np.full((SW_DH, SW_DH), 1.0 / SW_DH)), BF16)
    tile2 = lambda g: jnp.tile(g.astype(F32), LANES // SW_DH).reshape(1, LANES)
    q, kd, vd, z = pl.pallas_call(
        _sw_in_kernel,
        grid=(B, tt),
        in_specs=[_row_spec(tm, D_MODEL, tt), _const_spec((1, D_MODEL)), _mod_spec(layer),
                  _const_spec((D_MODEL, w_in.shape[1])), _row_spec(tm, 1, tt),
                  _const_spec((1, LANES)), _const_spec((1, LANES)), _const_spec((1, LANES)),
                  _const_spec((LANES, LANES))],
        out_specs=[_row_spec(tm, SW_QW, tt), _row_spec(tm, 4 * SW_KVW, tt),
                   _row_spec(tm, 2 * SW_KVW, tt), _row_spec(tm, SW_QW, tt)],
        out_shape=[jax.ShapeDtypeStruct((N, SW_QW), BF16), jax.ShapeDtypeStruct((N, 4 * SW_KVW), BF16),
                   jax.ShapeDtypeStruct((N, 2 * SW_KVW), BF16), jax.ShapeDtypeStruct((N, SW_QW), BF16)],
        compiler_params=_params(2),
        name=f"swa_in_{layer}",
    )(x2, norm_g.reshape(1, D_MODEL), mod, w_in.astype(BF16), positions.reshape(N, 1),
      invf, tile2(qn_g), tile2(kn_g), seg)

    nb = T // SW_BLOCK
    cur = lambda b, n: (b * nb + n, 0)
    prev = lambda b, n: (b * nb + jnp.maximum(n - 1, 0), 0)
    kw, vw = 4 * SW_KVW, 2 * SW_KVW
    o = pl.pallas_call(
        _sw_attn_kernel,
        grid=(B, nb),
        in_specs=[pl.BlockSpec(memory_space=pltpu.SMEM),
                  pl.BlockSpec((SW_BLOCK, SW_QW), cur),
                  pl.BlockSpec((SW_BLOCK, kw), cur), pl.BlockSpec((SW_BLOCK, kw), prev),
                  pl.BlockSpec((SW_BLOCK, vw), cur), pl.BlockSpec((SW_BLOCK, vw), prev)],
        out_specs=pl.BlockSpec((SW_BLOCK, SW_QW), cur),
        out_shape=jax.ShapeDtypeStruct((N, SW_QW), F32),
        compiler_params=_params(2),
        name=f"swa_attn_{layer}",
    )(sinks.astype(F32), q, kd, kd, vd, vd)
    return _out_proj(o, z, x2, mod, layer, w_out, B, T)


def _gd_in_kernel(x_ref, g_ref, mod_ref, w_ref, wab_hi_ref, wab_lo_ref, cw_ref, alog_ref, dtb_ref,
                  q_ref, k_ref, v_ref, z_ref, gb_ref, hist_ref):
    tm = x_ref.shape[0]
    H = SUBLANES

    @pl.when(pl.program_id(1) == 0)
    def _():
        hist_ref[0:H, :] = jnp.zeros((H, GD_QKV), F32)

    x = x_ref[...]
    b = pl.program_id(0)
    y = x * lax.rsqrt(jnp.mean(x * x, axis=-1, keepdims=True) + EPS) * g_ref[...]
    m = mod_ref[pl.ds(b, 1), :]
    hf = y * (1.0 + m[:, D_MODEL:2 * D_MODEL]) + m[:, :D_MODEL]
    h = hf.astype(BF16)
    h_lo = (hf - h.astype(F32)).astype(BF16)

    for s in range(GD_QKV // D_MODEL):
        cs = slice(s * D_MODEL, (s + 1) * D_MODEL)
        hist_ref[H:H + tm, cs] = _dot(h, w_ref[:, cs])
        conv = cw_ref[GD_CONV - 1:GD_CONV, cs] * hist_ref[H:H + tm, cs]
        for j in range(1, GD_CONV):
            conv = conv + cw_ref[GD_CONV - 1 - j:GD_CONV - j, cs] * hist_ref[H - j:H - j + tm, cs]
        act = _silu(conv)
        if s < 2:
            for c in range(D_MODEL // GD_DK):
                a = act[:, c * GD_DK:(c + 1) * GD_DK]
                r = lax.rsqrt(jnp.sum(a * a, axis=-1, keepdims=True) + EPS)
                if s == 0:
                    q_ref[:, c * GD_DK:(c + 1) * GD_DK] = (a * r * GD_DK ** -0.5).astype(BF16)
                else:
                    k_ref[:, c * GD_DK:(c + 1) * GD_DK] = (a * r).astype(BF16)
        else:
            v_ref[:, (s - 2) * D_MODEL:(s - 1) * D_MODEL] = act.astype(BF16)
    hist_ref[0:H, :] = hist_ref[tm:tm + H, :]

    z_ref[...] = _dot(h, w_ref[:, GD_QKV:GD_QKV + GD_VW]).astype(BF16)

    ab = _dot(h, wab_hi_ref[...]) + _dot(h, wab_lo_ref[...]) + _dot(h_lo, wab_hi_ref[...])
    xa = ab + dtb_ref[...]
    softplus = jnp.maximum(xa, 0.0) + jnp.log(1.0 + jnp.exp(-jnp.abs(xa)))
    gv = -jnp.exp(alog_ref[...]) * softplus
    lane = lax.broadcasted_iota(jnp.int32, (tm, LANES), 1)
    gb_ref[...] = jnp.where(lane < GD_V_HEADS, gv, _sigmoid(ab))


def _stack_lhs(x):
    hi = x.astype(BF16)
    hi32 = hi.astype(F32)
    c0 = (hi32 + pltpu.roll(x - hi32, GD_CHUNK, 1)).astype(BF16)
    return jnp.concatenate([c0, hi], axis=1)


def _stack_rhs(x):
    hi, lo = _split2(x)
    return jnp.concatenate([hi, hi, lo, jnp.zeros_like(hi)], axis=0)


def _gd_scan_kernel(q_ref, k_ref, v_ref, gb_ref, tri_ref, og_ref, o_ref, st_ref):
    C = GD_CHUNK

    @pl.when(pl.program_id(1) == 0)
    def _():
        st_ref[...] = jnp.zeros_like(st_ref)

    ti = lax.broadcasted_iota(jnp.int32, (C, LANES), 0)
    tj = lax.broadcasted_iota(jnp.int32, (C, LANES), 1)
    incl = tj <= ti
    strict = tj < ti
    eye = (ti == tj).astype(F32)
    gb = gb_ref[...]
    d_all = _dot_exact01(tri_ref[...], gb)
    og = og_ref[...]
    zrow = jnp.zeros((LANES - C, GD_DK), BF16)
    rep = GD_V_HEADS // GD_QK_HEADS
    heads = range(GD_V_HEADS)
    kk, qk = [], []
    for hq in range(GD_QK_HEADS):
        qs = slice(hq * GD_DK, (hq + 1) * GD_DK)
        kpad = jnp.concatenate([k_ref[:, qs], zrow], axis=0)
        kk.append(_dot_nt(k_ref[:, qs], kpad))
        qk.append(_dot_nt(q_ref[:, qs], kpad))
    beta, d_col, dec, t_inv, p = [], [], [], [], []
    for hv in heads:
        g_col = gb[:, hv:hv + 1]
        beta.append(gb[:, GD_V_HEADS + hv:GD_V_HEADS + hv + 1])
        d_col.append(d_all[:, hv:hv + 1])
        d_row = jnp.sum(jnp.where(ti <= tj, g_col, 0.0), axis=0, keepdims=True)
        dec.append(jnp.exp(jnp.where(incl, d_col[hv] - d_row, -jnp.inf)))
        a = kk[hv // rep] * beta[hv] * jnp.where(strict, dec[hv], 0.0)
        t_inv.append(eye - a)
        p.append(a)
    for _ in range(int(math.log2(C)) - 1):
        p = [_dot(_stack_lhs(p[hv]), _stack_rhs(p[hv])) for hv in heads]
        t_inv = [t_inv[hv] + _dot(_stack_lhs(t_inv[hv]), _stack_rhs(p[hv])) for hv in heads]
    uw, ed, kf = [], [], []
    for hv in heads:
        qs = slice((hv // rep) * GD_DK, (hv // rep + 1) * GD_DK)
        kf.append(k_ref[:, qs].astype(F32))
        ed.append(jnp.exp(d_col[hv]))
        v = v_ref[:, hv * GD_DV:(hv + 1) * GD_DV].astype(F32)
        rhs = jnp.concatenate([v * beta[hv], kf[hv] * beta[hv] * ed[hv]], axis=1).astype(BF16)
        rhs = jnp.concatenate([rhs, jnp.zeros_like(rhs)], axis=0)
        uw.append(_dot(t_inv[hv].astype(BF16), rhs))
    s16 = [st_ref[hv].astype(BF16) for hv in heads]
    vn16 = [(uw[hv][:, :GD_DV] - _dot(uw[hv][:, GD_DV:].astype(BF16), s16[hv])).astype(BF16) for hv in heads]
    for hv in heads:
        qs = slice((hv // rep) * GD_DK, (hv // rep + 1) * GD_DK)
        q = q_ref[:, qs].astype(F32)
        o = (_dot((q * ed[hv]).astype(BF16), s16[hv])
             + _dot((qk[hv // rep] * dec[hv]).astype(BF16), jnp.concatenate([vn16[hv], zrow], axis=0)))
        on = o * lax.rsqrt(jnp.mean(o * o, axis=-1, keepdims=True) + EPS) * og
        o_ref[:, hv * GD_DV:(hv + 1) * GD_DV] = on
    for hv in heads:
        d_last = d_col[hv][C - 1:C, :]
        kd = (kf[hv] * jnp.exp(d_last - d_col[hv])).astype(BF16)
        st_ref[hv] = st_ref[hv] * jnp.exp(d_last) + _dot_tn(kd, vn16[hv])


def _gdn_layer(x2, mod, layer, norm_g, w_in, w_out, conv_w, a_log, dt_bias, onorm_g, B, T):
    tm = ROW_TILE
    tt = T // tm
    N = B * T
    w_main = w_in[:, :GD_QKV + GD_VW].astype(BF16)
    w_ab = jnp.zeros((D_MODEL, LANES), F32).at[:, :2 * GD_V_HEADS].set(w_in[:, GD_QKV + GD_VW:])
    wab_hi = w_ab.astype(BF16)
    wab_lo = (w_ab - wab_hi.astype(F32)).astype(BF16)
    pad = lambda a: jnp.zeros((1, LANES), F32).at[0, :GD_V_HEADS].set(a.astype(F32))
    q, k, v, z, gb = pl.pallas_call(
        _gd_in_kernel,
        grid=(B, tt),
        in_specs=[_row_spec(tm, D_MODEL, tt), _const_spec((1, D_MODEL)), _mod_spec(layer),
                  _const_spec((D_MODEL, GD_QKV + GD_VW)), _const_spec((D_MODEL, LANES)),
                  _const_spec((D_MODEL, LANES)), _const_spec((GD_CONV, GD_QKV)),
                  _const_spec((1, LANES)), _const_spec((1, LANES))],
        out_specs=[_row_spec(tm, GD_QKW, tt), _row_spec(tm, GD_QKW, tt), _row_spec(tm, GD_VW, tt),
                   _row_spec(tm, GD_VW, tt), _row_spec(tm, LANES, tt)],
        out_shape=[jax.ShapeDtypeStruct((N, GD_QKW), BF16), jax.ShapeDtypeStruct((N, GD_QKW), BF16),
                   jax.ShapeDtypeStruct((N, GD_VW), BF16), jax.ShapeDtypeStruct((N, GD_VW), BF16),
                   jax.ShapeDtypeStruct((N, LANES), F32)],
        scratch_shapes=[pltpu.VMEM((tm + 2 * SUBLANES, GD_QKV), F32)],
        compiler_params=_params(2),
        name=f"gdn_in_{layer}",
    )(x2, norm_g.reshape(1, D_MODEL), mod, w_main, wab_hi, wab_lo, conv_w.astype(F32),
      pad(a_log), pad(dt_bias))

    C = GD_CHUNK
    tc = T // C
    tri = jnp.asarray(np.tril(np.ones((C, C), np.float32)), BF16)
    o = pl.pallas_call(
        _gd_scan_kernel,
        grid=(B, tc),
        in_specs=[_row_spec(C, GD_QKW, tc), _row_spec(C, GD_QKW, tc), _row_spec(C, GD_VW, tc),
                  _row_spec(C, LANES, tc), _const_spec((C, C)), _const_spec((1, GD_DV))],
        out_specs=_row_spec(C, GD_VW, tc),
        out_shape=jax.ShapeDtypeStruct((N, GD_VW), F32),
        scratch_shapes=[pltpu.VMEM((GD_V_HEADS, GD_DK, GD_DV), F32)],
        compiler_params=_params(2),
        name=f"gdn_scan_{layer}",
    )(q, k, v, gb, tri, onorm_g.reshape(1, GD_DV))
    return _out_proj(o, z, x2, mod, layer, w_out, B, T)


def kernel(x, c, positions, hgrn_lb, ada_w, ada_b, norm_g, hg_in_w, hg_out_w, hg_onorm, sw_in_w, sw_out_w,
           sw_qnorm, sw_knorm, sw_sinks, gd_in_w, gd_out_w, gd_conv_w, gd_a_log, gd_dt_bias, gd_onorm):
    B, T, _ = x.shape
    lb_all = jnp.cumsum(jax.nn.softmax(hgrn_lb.astype(F32), axis=0), axis=0)
    lb_all = lb_all - lb_all[0:1]
    mod = _ada_mod(c, ada_w, ada_b)
    x2 = x.reshape(B * T, D_MODEL)
    for i in range(DEPTH):
        j, kind = divmod(i, 3)
        if kind == 0:
            x2 = _hgrn2_layer(x2, mod, i, norm_g[i], lb_all[i], hg_in_w[j], hg_out_w[j], hg_onorm[j], B, T)
        elif kind == 1:
            x2 = _swa_layer(x2, mod, i, norm_g[i], positions, sw_in_w[j], sw_out_w[j], sw_qnorm[j],
                            sw_knorm[j], sw_sinks[j], B, T)
        else:
            x2 = _gdn_layer(x2, mod, i, norm_g[i], gd_in_w[j], gd_out_w[j], gd_conv_w[j], gd_a_log[j],
                            gd_dt_bias[j], gd_onorm[j], B, T)
    return x2.reshape(B, T, D_MODEL)
```

```python
import functools
import math

import numpy as np
import jax
import jax.numpy as jnp
from jax import lax
from jax.experimental import pallas as pl
from jax.experimental.pallas import tpu as pltpu

F32 = jnp.float32
BF16 = jnp.bfloat16

D_MODEL = 1024
DEPTH = 4
EPS = 1e-6
LANES = 128
SUBLANES = 8
VMEM_LIMIT = 56 * 2**20

HG_HEADS = 8
HG_DK = 128
HG_CHUNK = 64
HG_STEP_CHUNKS = 4

SW_HEADS = 16
SW_KV_HEADS = 4
SW_DH = 64
SW_BLOCK = 128
SW_QW = SW_HEADS * SW_DH
SW_KVW = SW_KV_HEADS * SW_DH
ROPE_THETA = 10000.0

GD_QK_HEADS = 8
GD_V_HEADS = 16
GD_DK = 128
GD_DV = 128
GD_CONV = 4
GD_CHUNK = 64
GD_QKW = GD_QK_HEADS * GD_DK
GD_VW = GD_V_HEADS * GD_DV
GD_QKV = 2 * GD_QKW + GD_VW

ROW_TILE = 512
GD_ROW_TILE = 512


def _params(n_axes):
    return pltpu.CompilerParams(dimension_semantics=("arbitrary",) * n_axes,
                                vmem_limit_bytes=VMEM_LIMIT)


def _dot(a, b):
    return jnp.dot(a, b, preferred_element_type=F32)


def _dot_nt(a, b):
    return lax.dot_general(a, b, (((1,), (1,)), ((), ())), preferred_element_type=F32)


def _dot_tn(a, b):
    return lax.dot_general(a, b, (((0,), (0,)), ((), ())), preferred_element_type=F32)


def _split2(x):
    hi = x.astype(BF16)
    lo = (x - hi.astype(F32)).astype(BF16)
    return hi, lo


def _dot_exact01(m01, x):
    hi, lo = _split2(x)
    return _dot(m01, hi) + _dot(m01, lo)


def _sigmoid(x):
    return 0.5 * jnp.tanh(0.5 * x) + 0.5


def _silu(x):
    h = 0.5 * x
    return h * jnp.tanh(h) + h


def _ada_kernel(c_ref, w_ref, b_ref, o_ref):
    o_ref[...] = jnp.dot(c_ref[...], w_ref[...], precision=lax.Precision.HIGHEST,
                         preferred_element_type=F32) + b_ref[...]


def _ada_mod(c, ada_w, ada_b):
    B = c.shape[0]
    depth = ada_w.shape[0]
    c8 = jnp.zeros((SUBLANES, D_MODEL), F32).at[:B].set(c)
    return pl.pallas_call(
        _ada_kernel,
        grid=(depth, 3),
        in_specs=[pl.BlockSpec((SUBLANES, D_MODEL), lambda i, j: (0, 0)),
                  pl.BlockSpec((None, D_MODEL, D_MODEL), lambda i, j: (i, 0, j)),
                  pl.BlockSpec((None, 1, D_MODEL), lambda i, j: (i, 0, j))],
        out_specs=pl.BlockSpec((None, SUBLANES, D_MODEL), lambda i, j: (i, 0, j)),
        out_shape=jax.ShapeDtypeStruct((depth, SUBLANES, 3 * D_MODEL), F32),
        compiler_params=_params(2),
        name="ada_mod",
    )(c8, ada_w, ada_b.reshape(depth, 1, 3 * D_MODEL))


def _prenorm(x_ref, g_ref, mod_ref):
    b = pl.program_id(0)
    x = x_ref[...]
    y = x * lax.rsqrt(jnp.mean(x * x, axis=-1, keepdims=True) + EPS) * g_ref[...]
    m = mod_ref[pl.ds(b, 1), :]
    shift = m[:, :D_MODEL]
    scale = m[:, D_MODEL:2 * D_MODEL]
    return (y * (1.0 + scale) + shift).astype(BF16)


def _row_spec(tm, width, t_tiles):
    return pl.BlockSpec((tm, width), lambda b, i: (b * t_tiles + i, 0))


def _const_spec(shape):
    return pl.BlockSpec(shape, lambda b, i: (0,) * len(shape))


def _mod_spec(layer):
    return pl.BlockSpec((None, SUBLANES, 3 * D_MODEL), lambda b, i: (layer, 0, 0))


def _out_kernel(o_ref, z_ref, x_ref, mod_ref, w_ref, out_ref):
    b = pl.program_id(0)
    z = z_ref[...].astype(F32)
    y = (o_ref[...].astype(F32) * _silu(z)).astype(BF16)
    acc = _dot(y, w_ref[...])
    gate = mod_ref[pl.ds(b, 1), :][:, 2 * D_MODEL:]
    out_ref[...] = x_ref[...] + gate * acc


def _out_proj(o, z, x2, mod, layer, w_out, B, T):
    tm = ROW_TILE
    tt = T // tm
    wo = w_out.shape[0]
    return pl.pallas_call(
        _out_kernel,
        grid=(B, tt),
        in_specs=[_row_spec(tm, wo, tt), _row_spec(tm, wo, tt), _row_spec(tm, D_MODEL, tt),
                  _mod_spec(layer), _const_spec((wo, D_MODEL))],
        out_specs=_row_spec(tm, D_MODEL, tt),
        out_shape=jax.ShapeDtypeStruct((B * T, D_MODEL), F32),
        compiler_params=_params(2),
        name=f"out_proj_{layer}",
    )(o, z, x2, mod, w_out.astype(BF16))


def _hg_in_kernel(x_ref, g_ref, mod_ref, w_ref, lb_ref, q_ref, k_ref, v_ref, z_ref, lf_ref):
    h = _prenorm(x_ref, g_ref, mod_ref)
    lb = lb_ref[...]
    qp = _dot(h, w_ref[:, 0:D_MODEL])
    q_ref[...] = _silu(qp).astype(BF16)
    fp = _dot(h, w_ref[:, D_MODEL:2 * D_MODEL])
    th = 0.5 * jnp.tanh(0.5 * fp)
    lf_ref[...] = jnp.log(lb + (1.0 - lb) * (0.5 + th))
    k_ref[...] = ((1.0 - lb) * (0.5 - th)).astype(BF16)
    v_ref[...] = _dot(h, w_ref[:, 2 * D_MODEL:3 * D_MODEL]).astype(BF16)
    z_ref[...] = _dot(h, w_ref[:, 3 * D_MODEL:4 * D_MODEL]).astype(BF16)


def _hg_levels(L):
    m, out = L // 2, []
    while m >= 1:
        out.append(m)
        m //= 2
    return out


def _hg_mxu_levels(L):
    return [m for m in _hg_levels(L) if 1 < m < SUBLANES]


def _hg_tables(L):
    t = np.arange(L)[:, None]
    u = np.arange(L)[None, :]
    masks = [np.eye(L, dtype=bool)]
    for m in _hg_levels(L):
        r = (t // (2 * m)) * (2 * m) + m
        ru = (u // (2 * m)) * (2 * m) + m
        masks.append((t // (2 * m) == u // (2 * m)) & (t >= r) & (u < ru))
    secs = [u <= t]
    for m in _hg_mxu_levels(L):
        r = (t // (2 * m)) * (2 * m) + m
        secs.append(np.where(t >= r, (u >= r) & (u <= t), (u > t) & (u < r)))
    cum = np.concatenate(secs, axis=0).astype(np.float32)
    return jnp.asarray(cum, BF16), jnp.asarray(np.stack(masks).astype(np.float32))


def _hg_exponents(lf, sums, ex_ref, b_ref, L, levels):
    G = SUBLANES
    width = lf.shape[1]
    mxu_levels = _hg_mxu_levels(L)
    b_ref[...] = sums[0:L]
    odd = (lax.broadcasted_iota(jnp.int32, (L, width), 0) & 1) != 0
    for li, m in enumerate(levels):
        dst = slice((2 + li) * L, (3 + li) * L)
        if m in mxu_levels:
            k = 1 + mxu_levels.index(m)
            ex_ref[dst, :] = jnp.exp(sums[k * L:(k + 1) * L])
        elif m == 1:
            ex_ref[dst, :] = jnp.exp(jnp.where(odd, lf, 0.0))
    rows = {}

    def brow(r):
        if r not in rows:
            rows[r] = jnp.broadcast_to(b_ref[r:r + 1, :], (G, width))
        return rows[r]

    for g in range(L // G):
        base = g * G
        b = b_ref[base:base + G, :]
        ex_ref[base:base + G, :] = jnp.exp(b)
        ex_ref[L + base:L + base + G, :] = jnp.exp(brow(L - 1) - b)
        for li, m in enumerate(levels):
            if m >= G:
                ref_row = (base // (2 * m)) * (2 * m) + m - 1
                d = b - brow(ref_row)
                ex_ref[(2 + li) * L + base:(2 + li) * L + base + G, :] = jnp.exp(d if base % (2 * m) >= m else -d)


def _hg_scan_kernel(q_ref, k_ref, v_ref, lf_ref, cum_ref, msk_ref, og_ref, o_ref, st_ref, ex_ref, b_ref, *, L):
    levels = _hg_levels(L)
    n_chunks = q_ref.shape[0] // L

    @pl.when(pl.program_id(1) == 0)
    def _():
        st_ref[...] = jnp.zeros_like(st_ref)

    row = lax.broadcasted_iota(jnp.int32, (L, HG_DK), 0)
    og = og_ref[...]
    heads = range(HG_HEADS)
    chunks = range(n_chunks)
    hs = [slice(h * HG_DK, (h + 1) * HG_DK) for h in heads]
    rs = [slice(c * L, (c + 1) * L) for c in chunks]
    for c in chunks:
        sums = _dot_exact01(cum_ref[...], lf_ref[rs[c], :])
        _hg_exponents(lf_ref[rs[c], :], sums, ex_ref.at[c], b_ref.at[c], L, levels)
    a = [[msk_ref[0] * _dot_nt(q_ref[rs[c], hs[h]], k_ref[rs[c], hs[h]]) for h in heads] for c in chunks]
    for li, m in enumerate(levels):
        upper = (row & m) != 0
        for c in chunks:
            for h in heads:
                e = ex_ref[c, (2 + li) * L:(3 + li) * L, hs[h]]
                x = jnp.where(upper, q_ref[rs[c], hs[h]].astype(F32), k_ref[rs[c], hs[h]].astype(F32))
                xm = (x * e).astype(BF16)
                a[c][h] = a[c][h] + msk_ref[li + 1] * _dot_nt(xm, xm)
    av = [[_dot(a[c][h].astype(BF16), v_ref[rs[c], hs[h]]) for h in heads] for c in chunks]
    for c in chunks:
        for h in heads:
            qd = (q_ref[rs[c], hs[h]].astype(F32) * ex_ref[c, 0:L, hs[h]]).astype(BF16)
            o = av[c][h] + _dot_nt(qd, st_ref[h].astype(BF16))
            on = o * lax.rsqrt(jnp.mean(o * o, axis=-1, keepdims=True) + EPS) * og
            o_ref[rs[c], hs[h]] = on.astype(o_ref.dtype)
        for h in heads:
            kd = (k_ref[rs[c], hs[h]].astype(F32) * ex_ref[c, L:2 * L, hs[h]]).astype(BF16)
            st_ref[h] = st_ref[h] * ex_ref[c, L - 1:L, hs[h]] + _dot_tn(v_ref[rs[c], hs[h]], kd)


def _hgrn2_layer(x2, mod, layer, norm_g, lb, w_in, w_out, onorm_g, B, T):
    tm = ROW_TILE
    tt = T // tm
    N = B * T
    q, k, v, z, lf = pl.pallas_call(
        _hg_in_kernel,
        grid=(B, tt),
        in_specs=[_row_spec(tm, D_MODEL, tt), _const_spec((1, D_MODEL)), _mod_spec(layer),
                  _const_spec((D_MODEL, 4 * D_MODEL)), _const_spec((1, D_MODEL))],
        out_specs=[_row_spec(tm, D_MODEL, tt)] * 5,
        out_shape=[jax.ShapeDtypeStruct((N, D_MODEL), BF16)] * 4 + [jax.ShapeDtypeStruct((N, D_MODEL), F32)],
        compiler_params=_params(2),
        name=f"hgrn2_in_{layer}",
    )(x2, norm_g.reshape(1, D_MODEL), mod, w_in.astype(BF16), lb.reshape(1, D_MODEL))

    L = HG_CHUNK
    nc = HG_STEP_CHUNKS
    tc = T // (nc * L)
    cum, msk = _hg_tables(L)
    o = pl.pallas_call(
        functools.partial(_hg_scan_kernel, L=L),
        grid=(B, tc),
        in_specs=[_row_spec(nc * L, D_MODEL, tc)] * 4 + [_const_spec(cum.shape), _const_spec(msk.shape),
                                                          _const_spec((1, HG_DK))],
        out_specs=_row_spec(nc * L, D_MODEL, tc),
        out_shape=jax.ShapeDtypeStruct((N, D_MODEL), BF16),
        scratch_shapes=[pltpu.VMEM((HG_HEADS, HG_DK, HG_DK), F32),
                        pltpu.VMEM((nc, (2 + len(_hg_levels(L))) * L, D_MODEL), F32),
                        pltpu.VMEM((nc, L, D_MODEL), F32)],
        compiler_params=_params(2),
        name=f"hgrn2_scan_{layer}",
    )(q, k, v, lf, cum, msk, onorm_g.reshape(1, HG_DK))
    return _out_proj(o, z, x2, mod, layer, w_out, B, T)


def _sw_in_kernel(x_ref, g_ref, mod_ref, w_ref, pos_ref, invf_ref, qg_ref, kg_ref, seg_ref,
                  q_ref, kd_ref, vd_ref, z_ref):
    h = _prenorm(x_ref, g_ref, mod_ref)
    tm = h.shape[0]
    ang = invf_ref[...] * pos_ref[...].astype(F32)
    cos_f = jnp.cos(ang)
    sin_f = jnp.sin(ang)
    cs = jnp.concatenate([cos_f, sin_f, cos_f, sin_f], axis=0).T
    lane = lax.broadcasted_iota(jnp.int32, (tm, LANES), 1)
    first = (lane & (SW_DH // 2)) == 0
    cos_t = jnp.where(first, cs, pltpu.roll(cs, SW_DH // 2, 1))
    sin_a = jnp.where(first, -pltpu.roll(cs, LANES - SW_DH // 2, 1), 0.0)
    sin_b = jnp.where(first, 0.0, cs)
    low = lane < SW_DH
    seg = seg_ref[...]
    W2 = 2 * LANES

    xq = _dot(h, w_ref[:, 0:SW_QW])
    xk = _dot(h, w_ref[:, SW_QW:SW_QW + SW_KVW])
    xv = _dot(h, w_ref[:, SW_QW + SW_KVW:SW_QW + 2 * SW_KVW])
    z_ref[...] = _dot(h, w_ref[:, SW_QW + 2 * SW_KVW:]).astype(BF16)
    ms_q = [_dot((xq[:, c * W2:(c + 1) * W2] * xq[:, c * W2:(c + 1) * W2]).astype(BF16), seg)
            for c in range(SW_QW // W2)]
    ms_k = [_dot((xk[:, c * W2:(c + 1) * W2] * xk[:, c * W2:(c + 1) * W2]).astype(BF16), seg)
            for c in range(SW_KVW // W2)]

    def norm_rope(xg, ms, gain):
        xn = xg * lax.rsqrt(ms + EPS) * gain
        return (xn * cos_t + pltpu.roll(xn, LANES - SW_DH // 2, 1) * sin_a
                + pltpu.roll(xn, SW_DH // 2, 1) * sin_b)

    def group(x, ms, c):
        half = (c % 2) * LANES
        return x[:, c * LANES:(c + 1) * LANES], ms[c // 2][:, half:half + LANES]

    scale = SW_DH ** -0.5
    for c in range(SW_QW // LANES):
        xg, ms = group(xq, ms_q, c)
        q_ref[:, c * LANES:(c + 1) * LANES] = (norm_rope(xg, ms, qg_ref[...]) * scale).astype(BF16)
    for c in range(SW_KVW // LANES):
        xg, ms = group(xk, ms_k, c)
        kr = norm_rope(xg, ms, kg_ref[...])
        sw = pltpu.roll(kr, SW_DH, 1)
        for hh, (in_low, in_high) in enumerate(((kr, sw), (sw, kr))):
            base = (2 * c + hh) * 2 * LANES
            kd_ref[:, base:base + LANES] = jnp.where(low, in_low, 0.0).astype(BF16)
            kd_ref[:, base + LANES:base + 2 * LANES] = jnp.where(low, 0.0, in_high).astype(BF16)
        vg = xv[:, c * LANES:(c + 1) * LANES]
        sv = pltpu.roll(vg, SW_DH, 1)
        vd_ref[:, (2 * c) * LANES:(2 * c + 1) * LANES] = jnp.where(low, vg, sv).astype(BF16)
        vd_ref[:, (2 * c + 1) * LANES:(2 * c + 2) * LANES] = jnp.where(low, sv, vg).astype(BF16)


def _sw_attn_kernel(sink_ref, q_ref, kc_ref, kp_ref, vc_ref, vp_ref, o_ref):
    n = pl.program_id(1)
    BLK = SW_BLOCK
    qi = lax.broadcasted_iota(jnp.int32, (BLK, 2 * BLK), 0)
    kj = lax.broadcasted_iota(jnp.int32, (BLK, 2 * BLK), 1)
    allowed = ((kj < BLK) & (kj > qi) & (n > 0)) | ((kj >= BLK) & ((kj - BLK) <= qi))
    bias = jnp.where(allowed, 0.0, -jnp.inf)
    lane = lax.broadcasted_iota(jnp.int32, (BLK, LANES), 1)
    low = lane < SW_DH
    group = SW_HEADS // SW_KV_HEADS
    for h in range(SW_KV_HEADS):
        hs = slice(h * LANES, (h + 1) * LANES)
        vv = jnp.concatenate([vp_ref[:, hs], vc_ref[:, hs]], axis=0)
        kks = []
        for half in range(2):
            ks = slice((2 * h + half) * LANES, (2 * h + half + 1) * LANES)
            kks.append(jnp.concatenate([kp_ref[:, ks], kc_ref[:, ks]], axis=0))
        for j in range(group // 2):
            c = h * (group // 2) + j
            qp = q_ref[:, c * LANES:(c + 1) * LANES]
            outs = []
            for half in range(2):
                s = _dot_nt(qp, kks[half]) + bias
                sink = sink_ref[2 * c + half]
                m = jnp.maximum(jnp.max(s, axis=-1, keepdims=True), sink)
                p = jnp.exp(s - m)
                l = jnp.sum(p, axis=-1, keepdims=True) + jnp.exp(sink - m)
                outs.append(_dot(p.astype(BF16), vv) / l)
            o_ref[:, c * LANES:(c + 1) * LANES] = jnp.where(low, outs[0], outs[1]).astype(o_ref.dtype)


def _swa_layer(x2, mod, layer, norm_g, positions, w_in, w_out, qn_g, kn_g, sinks, B, T):
    tm = ROW_TILE
    tt = T // tm
    N = B * T
    inv_freq = ROPE_THETA ** (-jnp.arange(0, SW_DH, 2, dtype=F32) / SW_DH)
    invf = inv_freq.reshape(SW_DH // 2, 1)
    seg = jnp.asarray(np.kron(np.eye(2 * LANES // SW_DH), np.full((SW_DH, SW_DH), 1.0 / SW_DH)), BF16)
    tile2 = lambda g: jnp.tile(g.astype(F32), LANES // SW_DH).reshape(1, LANES)
    q, kd, vd, z = pl.pallas_call(
        _sw_in_kernel,
        grid=(B, tt),
        in_specs=[_row_spec(tm, D_MODEL, tt), _const_spec((1, D_MODEL)), _mod_spec(layer),
                  _const_spec((D_MODEL, w_in.shape[1])),
                  pl.BlockSpec((None, 1, tm), lambda b, i: (b * tt + i, 0, 0)),
                  _const_spec((SW_DH // 2, 1)), _const_spec((1, LANES)), _const_spec((1, LANES)),
                  _const_spec((2 * LANES, 2 * LANES))],
        out_specs=[_row_spec(tm, SW_QW, tt), _row_spec(tm, 4 * SW_KVW, tt),
                   _row_spec(tm, 2 * SW_KVW, tt), _row_spec(tm, SW_QW, tt)],
        out_shape=[jax.ShapeDtypeStruct((N, SW_QW), BF16), jax.ShapeDtypeStruct((N, 4 * SW_KVW), BF16),
                   jax.ShapeDtypeStruct((N, 2 * SW_KVW), BF16), jax.ShapeDtypeStruct((N, SW_QW), BF16)],
        compiler_params=_params(2),
        name=f"swa_in_{layer}",
    )(x2, norm_g.reshape(1, D_MODEL), mod, w_in.astype(BF16), positions.reshape(N // tm, 1, tm),
      invf, tile2(qn_g), tile2(kn_g), seg)

    nb = T // SW_BLOCK
    cur = lambda b, n: (b * nb + n, 0)
    prev = lambda b, n: (b * nb + jnp.maximum(n - 1, 0), 0)
    kw, vw = 4 * SW_KVW, 2 * SW_KVW
    o = pl.pallas_call(
        _sw_attn_kernel,
        grid=(B, nb),
        in_specs=[pl.BlockSpec(memory_space=pltpu.SMEM),
                  pl.BlockSpec((SW_BLOCK, SW_QW), cur),
                  pl.BlockSpec((SW_BLOCK, kw), cur), pl.BlockSpec((SW_BLOCK, kw), prev),
                  pl.BlockSpec((SW_BLOCK, vw), cur), pl.BlockSpec((SW_BLOCK, vw), prev)],
        out_specs=pl.BlockSpec((SW_BLOCK, SW_QW), cur),
        out_shape=jax.ShapeDtypeStruct((N, SW_QW), BF16),
        compiler_params=_params(2),
        name=f"swa_attn_{layer}",
    )(sinks.astype(F32), q, kd, kd, vd, vd)
    return _out_proj(o, z, x2, mod, layer, w_out, B, T)


def _gd_in_kernel(x_ref, g_ref, mod_ref, w_ref, wab_hi_ref, wab_lo_ref, cw_ref, alog_ref, dtb_ref,
                  q_ref, k_ref, v_ref, z_ref, gb_ref, hist_ref):
    tm = x_ref.shape[0]
    H = SUBLANES

    @pl.when(pl.program_id(1) == 0)
    def _():
        hist_ref[...] = jnp.zeros_like(hist_ref)

    x = x_ref[...]
    b = pl.program_id(0)
    y = x * lax.rsqrt(jnp.mean(x * x, axis=-1, keepdims=True) + EPS) * g_ref[...]
    m = mod_ref[pl.ds(b, 1), :]
    hf = y * (1.0 + m[:, D_MODEL:2 * D_MODEL]) + m[:, :D_MODEL]
    h = hf.astype(BF16)
    h_lo = (hf - h.astype(F32)).astype(BF16)

    for s in range(GD_QKV // D_MODEL):
        cs = slice(s * D_MODEL, (s + 1) * D_MODEL)
        cur = _dot(h, w_ref[:, cs])
        taps = [cw_ref[GD_CONV - 1 - j:GD_CONV - j, cs] for j in range(GD_CONV)]
        grp = jnp.concatenate([hist_ref[:, cs], cur], axis=0).reshape(tm // H + 1, H, D_MODEL)
        sub = lax.broadcasted_iota(jnp.int32, (tm // H, H, D_MODEL), 1)
        conv = taps[0] * cur
        for j in range(1, GD_CONV):
            rot = pltpu.roll(grp, j, 1)
            xj = jnp.where(sub < j, rot[:-1], rot[1:]).reshape(tm, D_MODEL)
            conv = conv + taps[j] * xj
        hist_ref[:, cs] = cur[tm - H:tm]
        act = _silu(conv)
        if s < 2:
            for c in range(D_MODEL // GD_DK):
                a = act[:, c * GD_DK:(c + 1) * GD_DK]
                r = lax.rsqrt(jnp.sum(a * a, axis=-1, keepdims=True) + EPS)
                if s == 0:
                    q_ref[:, c * GD_DK:(c + 1) * GD_DK] = (a * r * GD_DK ** -0.5).astype(BF16)
                else:
                    k_ref[:, c * GD_DK:(c + 1) * GD_DK] = (a * r).astype(BF16)
        else:
            v_ref[:, (s - 2) * D_MODEL:(s - 1) * D_MODEL] = act.astype(BF16)
    z_ref[...] = _dot(h, w_ref[:, GD_QKV:GD_QKV + GD_VW]).astype(BF16)

    ab = _dot(h, wab_hi_ref[...]) + _dot(h, wab_lo_ref[...]) + _dot(h_lo, wab_hi_ref[...])
    xa = ab + dtb_ref[...]
    softplus = jnp.maximum(xa, 0.0) + jnp.log(1.0 + jnp.exp(-jnp.abs(xa)))
    gv = -jnp.exp(alog_ref[...]) * softplus
    lane = lax.broadcasted_iota(jnp.int32, (tm, LANES), 1)
    gb_ref[...] = jnp.where(lane < GD_V_HEADS, gv, _sigmoid(ab))


def _stack_lhs(x):
    hi = x.astype(BF16)
    hi32 = hi.astype(F32)
    c0 = (hi32 + pltpu.roll(x - hi32, GD_CHUNK, 1)).astype(BF16)
    return jnp.concatenate([c0, hi], axis=1)


def _stack_rhs(x):
    hi, lo = _split2(x)
    return jnp.concatenate([hi, hi, lo, jnp.zeros_like(hi)], axis=0)


def _gd_scan_kernel(q_ref, k_ref, v_ref, gb_ref, tri_ref, og_ref, o_ref, st_ref):
    C = GD_CHUNK

    @pl.when(pl.program_id(1) == 0)
    def _():
        st_ref[...] = jnp.zeros_like(st_ref)

    ti = lax.broadcasted_iota(jnp.int32, (C, LANES), 0)
    tj = lax.broadcasted_iota(jnp.int32, (C, LANES), 1)
    incl = tj <= ti
    strict = tj < ti
    eye = (ti == tj).astype(F32)
    gb = gb_ref[...]
    d_all = _dot_exact01(tri_ref[...], gb)
    og = og_ref[...]
    zrow = jnp.zeros((LANES - C, GD_DK), BF16)
    rep = GD_V_HEADS // GD_QK_HEADS
    heads = range(GD_V_HEADS)
    kk, qk = [], []
    for hq in range(GD_QK_HEADS):
        qs = slice(hq * GD_DK, (hq + 1) * GD_DK)
        kpad = jnp.concatenate([k_ref[:, qs], zrow], axis=0)
        kk.append(_dot_nt(k_ref[:, qs], kpad))
        qk.append(_dot_nt(q_ref[:, qs], kpad))
    beta, d_col, dec, t_inv, p = [], [], [], [], []
    for hv in heads:
        g_col = gb[:, hv:hv + 1]
        beta.append(gb[:, GD_V_HEADS + hv:GD_V_HEADS + hv + 1])
        d_col.append(d_all[:, hv:hv + 1])
        d_row = jnp.sum(jnp.where(ti <= tj, g_col, 0.0), axis=0, keepdims=True)
        dec.append(jnp.exp(jnp.where(incl, d_col[hv] - d_row, -jnp.inf)))
        a = kk[hv // rep] * beta[hv] * jnp.where(strict, dec[hv], 0.0)
        t_inv.append(eye - a)
        p.append(a)
    for _ in range(int(math.log2(C)) - 1):
        p = [_dot(_stack_lhs(p[hv]), _stack_rhs(p[hv])) for hv in heads]
        t_inv = [t_inv[hv] + _dot(_stack_lhs(t_inv[hv]), _stack_rhs(p[hv])) for hv in heads]
    uw, ed, kf = [], [], []
    for hv in heads:
        qs = slice((hv // rep) * GD_DK, (hv // rep + 1) * GD_DK)
        kf.append(k_ref[:, qs].astype(F32))
        ed.append(jnp.exp(d_col[hv]))
        v = v_ref[:, hv * GD_DV:(hv + 1) * GD_DV].astype(F32)
        rhs = jnp.concatenate([v * beta[hv], kf[hv] * beta[hv] * ed[hv]], axis=1).astype(BF16)
        rhs = jnp.concatenate([rhs, jnp.zeros_like(rhs)], axis=0)
        uw.append(_dot(t_inv[hv].astype(BF16), rhs))
    s16 = [st_ref[hv].astype(BF16) for hv in heads]
    vn16 = [(uw[hv][:, :GD_DV] - _dot(uw[hv][:, GD_DV:].astype(BF16), s16[hv])).astype(BF16) for hv in heads]
    for hv in heads:
        qs = slice((hv // rep) * GD_DK, (hv // rep + 1) * GD_DK)
        q = q_ref[:, qs].astype(F32)
        o = (_dot((q * ed[hv]).astype(BF16), s16[hv])
             + _dot((qk[hv // rep] * dec[hv]).astype(BF16), jnp.concatenate([vn16[hv], zrow], axis=0)))
        on = o * lax.rsqrt(jnp.mean(o * o, axis=-1, keepdims=True) + EPS) * og
        o_ref[:, hv * GD_DV:(hv + 1) * GD_DV] = on.astype(o_ref.dtype)
    for hv in heads:
        d_last = d_col[hv][C - 1:C, :]
        kd = (kf[hv] * jnp.exp(d_last - d_col[hv])).astype(BF16)
        st_ref[hv] = st_ref[hv] * jnp.exp(d_last) + _dot_tn(kd, vn16[hv])


def _gdn_layer(x2, mod, layer, norm_g, w_in, w_out, conv_w, a_log, dt_bias, onorm_g, B, T):
    tm = GD_ROW_TILE
    tt = T // tm
    N = B * T
    w_main = w_in[:, :GD_QKV + GD_VW].astype(BF16)
    w_ab = jnp.zeros((D_MODEL, LANES), F32).at[:, :2 * GD_V_HEADS].set(w_in[:, GD_QKV + GD_VW:])
    wab_hi = w_ab.astype(BF16)
    wab_lo = (w_ab - wab_hi.astype(F32)).astype(BF16)
    pad = lambda a: jnp.zeros((1, LANES), F32).at[0, :GD_V_HEADS].set(a.astype(F32))
    q, k, v, z, gb = pl.pallas_call(
        _gd_in_kernel,
        grid=(B, tt),
        in_specs=[_row_spec(tm, D_MODEL, tt), _const_spec((1, D_MODEL)), _mod_spec(layer),
                  _const_spec((D_MODEL, GD_QKV + GD_VW)), _const_spec((D_MODEL, LANES)),
                  _const_spec((D_MODEL, LANES)), _const_spec((GD_CONV, GD_QKV)),
                  _const_spec((1, LANES)), _const_spec((1, LANES))],
        out_specs=[_row_spec(tm, GD_QKW, tt), _row_spec(tm, GD_QKW, tt), _row_spec(tm, GD_VW, tt),
                   _row_spec(tm, GD_VW, tt), _row_spec(tm, LANES, tt)],
        out_shape=[jax.ShapeDtypeStruct((N, GD_QKW), BF16), jax.ShapeDtypeStruct((N, GD_QKW), BF16),
                   jax.ShapeDtypeStruct((N, GD_VW), BF16), jax.ShapeDtypeStruct((N, GD_VW), BF16),
                   jax.ShapeDtypeStruct((N, LANES), F32)],
        scratch_shapes=[pltpu.VMEM((SUBLANES, GD_QKV), F32)],
        compiler_params=_params(2),
        name=f"gdn_in_{layer}",
    )(x2, norm_g.reshape(1, D_MODEL), mod, w_main, wab_hi, wab_lo, conv_w.astype(F32),
      pad(a_log), pad(dt_bias))

    C = GD_CHUNK
    tc = T // C
    tri = jnp.asarray(np.tril(np.ones((C, C), np.float32)), BF16)
    o = pl.pallas_call(
        _gd_scan_kernel,
        grid=(B, tc),
        in_specs=[_row_spec(C, GD_QKW, tc), _row_spec(C, GD_QKW, tc), _row_spec(C, GD_VW, tc),
                  _row_spec(C, LANES, tc), _const_spec((C, C)), _const_spec((1, GD_DV))],
        out_specs=_row_spec(C, GD_VW, tc),
        out_shape=jax.ShapeDtypeStruct((N, GD_VW), BF16),
        scratch_shapes=[pltpu.VMEM((GD_V_HEADS, GD_DK, GD_DV), F32)],
        compiler_params=_params(2),
        name=f"gdn_scan_{layer}",
    )(q, k, v, gb, tri, onorm_g.reshape(1, GD_DV))
    return _out_proj(o, z, x2, mod, layer, w_out, B, T)


def kernel(x, c, positions, hgrn_lb, ada_w, ada_b, norm_g, hg_in_w, hg_out_w, hg_onorm, sw_in_w, sw_out_w,
           sw_qnorm, sw_knorm, sw_sinks, gd_in_w, gd_out_w, gd_conv_w, gd_a_log, gd_dt_bias, gd_onorm):
    B, T, _ = x.shape
    lb_all = jnp.cumsum(jax.nn.softmax(hgrn_lb.astype(F32), axis=0), axis=0)
    lb_all = lb_all - lb_all[0:1]
    mod = _ada_mod(c, ada_w, ada_b)
    x2 = x.reshape(B * T, D_MODEL)
    for i in range(DEPTH):
        j, kind = divmod(i, 3)
        if kind == 0:
            x2 = _hgrn2_layer(x2, mod, i, norm_g[i], lb_all[i], hg_in_w[j], hg_out_w[j], hg_onorm[j], B, T)
        elif kind == 1:
            x2 = _swa_layer(x2, mod, i, norm_g[i], positions, sw_in_w[j], sw_out_w[j], sw_qnorm[j],
                            sw_knorm[j], sw_sinks[j], B, T)
        else:
            x2 = _gdn_layer(x2, mod, i, norm_g[i], gd_in_w[j], gd_out_w[j], gd_conv_w[j], gd_a_log[j],
                            gd_dt_bias[j], gd_onorm[j], B, T)
    return x2.reshape(B, T, D_MODEL)
```

```python
import functools
import math

import numpy as np
import jax
import jax.numpy as jnp
from jax import lax
from jax.experimental import pallas as pl
from jax.experimental.pallas import tpu as pltpu

F32 = jnp.float32
BF16 = jnp.bfloat16

D_MODEL = 1024
DEPTH = 4
EPS = 1e-6
LANES = 128
SUBLANES = 8
VMEM_LIMIT = 56 * 2**20

HG_HEADS = 8
HG_DK = 128
HG_CHUNK = 64
HG_STEP_CHUNKS = 4

SW_HEADS = 16
SW_KV_HEADS = 4
SW_DH = 64
SW_BLOCK = 128
SW_QW = SW_HEADS * SW_DH
SW_KVW = SW_KV_HEADS * SW_DH
ROPE_THETA = 10000.0

GD_QK_HEADS = 8
GD_V_HEADS = 16
GD_DK = 128
GD_DV = 128
GD_CONV = 4
GD_CHUNK = 64
GD_STEP_CHUNKS = 2
GD_QKW = GD_QK_HEADS * GD_DK
GD_VW = GD_V_HEADS * GD_DV
GD_QKV = 2 * GD_QKW + GD_VW

ROW_TILE = 512
GD_ROW_TILE = 512


def _params(n_axes):
    return pltpu.CompilerParams(dimension_semantics=("arbitrary",) * n_axes,
                                vmem_limit_bytes=VMEM_LIMIT)


def _dot(a, b):
    return jnp.dot(a, b, preferred_element_type=F32)


def _dot_nt(a, b):
    return lax.dot_general(a, b, (((1,), (1,)), ((), ())), preferred_element_type=F32)


def _dot_tn(a, b):
    return lax.dot_general(a, b, (((0,), (0,)), ((), ())), preferred_element_type=F32)


def _split2(x):
    hi = x.astype(BF16)
    lo = (x - hi.astype(F32)).astype(BF16)
    return hi, lo


def _dot_exact01(m01, x):
    hi, lo = _split2(x)
    return _dot(m01, hi) + _dot(m01, lo)


def _sigmoid(x):
    return 0.5 * jnp.tanh(0.5 * x) + 0.5


def _silu(x):
    h = 0.5 * x
    return h * jnp.tanh(h) + h


def _ada_kernel(c_ref, w_ref, b_ref, o_ref):
    o_ref[...] = jnp.dot(c_ref[...], w_ref[...], precision=lax.Precision.HIGHEST,
                         preferred_element_type=F32) + b_ref[...]


def _ada_mod(c, ada_w, ada_b):
    B = c.shape[0]
    depth = ada_w.shape[0]
    c8 = jnp.zeros((SUBLANES, D_MODEL), F32).at[:B].set(c)
    return pl.pallas_call(
        _ada_kernel,
        grid=(depth, 3),
        in_specs=[pl.BlockSpec((SUBLANES, D_MODEL), lambda i, j: (0, 0)),
                  pl.BlockSpec((None, D_MODEL, D_MODEL), lambda i, j: (i, 0, j)),
                  pl.BlockSpec((None, 1, D_MODEL), lambda i, j: (i, 0, j))],
        out_specs=pl.BlockSpec((None, SUBLANES, D_MODEL), lambda i, j: (i, 0, j)),
        out_shape=jax.ShapeDtypeStruct((depth, SUBLANES, 3 * D_MODEL), F32),
        compiler_params=_params(2),
        name="ada_mod",
    )(c8, ada_w, ada_b.reshape(depth, 1, 3 * D_MODEL))


def _prenorm(x_ref, g_ref, mod_ref):
    b = pl.program_id(0)
    x = x_ref[...]
    y = x * lax.rsqrt(jnp.mean(x * x, axis=-1, keepdims=True) + EPS) * g_ref[...]
    m = mod_ref[pl.ds(b, 1), :]
    shift = m[:, :D_MODEL]
    scale = m[:, D_MODEL:2 * D_MODEL]
    return (y * (1.0 + scale) + shift).astype(BF16)


def _row_spec(tm, width, t_tiles):
    return pl.BlockSpec((tm, width), lambda b, i: (b * t_tiles + i, 0))


def _const_spec(shape):
    return pl.BlockSpec(shape, lambda b, i: (0,) * len(shape))


def _mod_spec(layer):
    return pl.BlockSpec((None, SUBLANES, 3 * D_MODEL), lambda b, i: (layer, 0, 0))


def _out_kernel(o_ref, z_ref, x_ref, mod_ref, w_ref, out_ref):
    b = pl.program_id(0)
    z = z_ref[...].astype(F32)
    y = (o_ref[...].astype(F32) * _silu(z)).astype(BF16)
    acc = _dot(y, w_ref[...])
    gate = mod_ref[pl.ds(b, 1), :][:, 2 * D_MODEL:]
    out_ref[...] = x_ref[...] + gate * acc


def _out_proj(o, z, x2, mod, layer, w_out, B, T):
    tm = ROW_TILE
    tt = T // tm
    wo = w_out.shape[0]
    return pl.pallas_call(
        _out_kernel,
        grid=(B, tt),
        in_specs=[_row_spec(tm, wo, tt), _row_spec(tm, wo, tt), _row_spec(tm, D_MODEL, tt),
                  _mod_spec(layer), _const_spec((wo, D_MODEL))],
        out_specs=_row_spec(tm, D_MODEL, tt),
        out_shape=jax.ShapeDtypeStruct((B * T, D_MODEL), F32),
        compiler_params=_params(2),
        name=f"out_proj_{layer}",
    )(o, z, x2, mod, w_out.astype(BF16))


def _hg_in_kernel(x_ref, g_ref, mod_ref, w_ref, lb_ref, q_ref, k_ref, v_ref, z_ref, lf_ref):
    h = _prenorm(x_ref, g_ref, mod_ref)
    lb = lb_ref[...]
    qp = _dot(h, w_ref[:, 0:D_MODEL])
    q_ref[...] = _silu(qp).astype(BF16)
    fp = _dot(h, w_ref[:, D_MODEL:2 * D_MODEL])
    th = 0.5 * jnp.tanh(0.5 * fp)
    lf_ref[...] = jnp.log(lb + (1.0 - lb) * (0.5 + th))
    k_ref[...] = ((1.0 - lb) * (0.5 - th)).astype(BF16)
    v_ref[...] = _dot(h, w_ref[:, 2 * D_MODEL:3 * D_MODEL]).astype(BF16)
    z_ref[...] = _dot(h, w_ref[:, 3 * D_MODEL:4 * D_MODEL]).astype(BF16)


def _hg_levels(L):
    m, out = L // 2, []
    while m >= 1:
        out.append(m)
        m //= 2
    return out


def _hg_mxu_levels(L):
    return [m for m in _hg_levels(L) if 1 < m < SUBLANES]


def _hg_tables(L):
    t = np.arange(L)[:, None]
    u = np.arange(L)[None, :]
    masks = [np.eye(L, dtype=bool)]
    for m in _hg_levels(L):
        r = (t // (2 * m)) * (2 * m) + m
        ru = (u // (2 * m)) * (2 * m) + m
        masks.append((t // (2 * m) == u // (2 * m)) & (t >= r) & (u < ru))
    secs = [u <= t]
    for m in _hg_mxu_levels(L):
        r = (t // (2 * m)) * (2 * m) + m
        secs.append(np.where(t >= r, (u >= r) & (u <= t), (u > t) & (u < r)))
    cum = np.concatenate(secs, axis=0).astype(np.float32)
    return jnp.asarray(cum, BF16), jnp.asarray(np.stack(masks).astype(np.float32))


def _hg_exponents(lf, sums, ex_ref, b_ref, L, levels):
    G = SUBLANES
    width = lf.shape[1]
    mxu_levels = _hg_mxu_levels(L)
    b_ref[...] = sums[0:L]
    odd = (lax.broadcasted_iota(jnp.int32, (L, width), 0) & 1) != 0
    for li, m in enumerate(levels):
        dst = slice((2 + li) * L, (3 + li) * L)
        if m in mxu_levels:
            k = 1 + mxu_levels.index(m)
            ex_ref[dst, :] = jnp.exp(sums[k * L:(k + 1) * L])
        elif m == 1:
            ex_ref[dst, :] = jnp.exp(jnp.where(odd, lf, 0.0))
    rows = {}

    def brow(r):
        if r not in rows:
            rows[r] = jnp.broadcast_to(b_ref[r:r + 1, :], (G, width))
        return rows[r]

    for g in range(L // G):
        base = g * G
        b = b_ref[base:base + G, :]
        ex_ref[base:base + G, :] = jnp.exp(b)
        ex_ref[L + base:L + base + G, :] = jnp.exp(brow(L - 1) - b)
        for li, m in enumerate(levels):
            if m >= G:
                ref_row = (base // (2 * m)) * (2 * m) + m - 1
                d = b - brow(ref_row)
                ex_ref[(2 + li) * L + base:(2 + li) * L + base + G, :] = jnp.exp(d if base % (2 * m) >= m else -d)


def _hg_scan_kernel(q_ref, k_ref, v_ref, lf_ref, cum_ref, msk_ref, og_ref, o_ref, st_ref, ex_ref, b_ref, *, L):
    levels = _hg_levels(L)
    n_chunks = q_ref.shape[0] // L

    @pl.when(pl.program_id(1) == 0)
    def _():
        st_ref[...] = jnp.zeros_like(st_ref)

    row = lax.broadcasted_iota(jnp.int32, (L, HG_DK), 0)
    og = og_ref[...]
    heads = range(HG_HEADS)
    chunks = range(n_chunks)
    hs = [slice(h * HG_DK, (h + 1) * HG_DK) for h in heads]
    rs = [slice(c * L, (c + 1) * L) for c in chunks]
    for c in chunks:
        sums = _dot_exact01(cum_ref[...], lf_ref[rs[c], :])
        _hg_exponents(lf_ref[rs[c], :], sums, ex_ref.at[c], b_ref.at[c], L, levels)
    a = [[msk_ref[0] * _dot_nt(q_ref[rs[c], hs[h]], k_ref[rs[c], hs[h]]) for h in heads] for c in chunks]
    for li, m in enumerate(levels):
        upper = (row & m) != 0
        for c in chunks:
            for h in heads:
                e = ex_ref[c, (2 + li) * L:(3 + li) * L, hs[h]]
                x = jnp.where(upper, q_ref[rs[c], hs[h]].astype(F32), k_ref[rs[c], hs[h]].astype(F32))
                xm = (x * e).astype(BF16)
                a[c][h] = a[c][h] + msk_ref[li + 1] * _dot_nt(xm, xm)
    av = [[_dot(a[c][h].astype(BF16), v_ref[rs[c], hs[h]]) for h in heads] for c in chunks]
    for c in chunks:
        for h in heads:
            qd = (q_ref[rs[c], hs[h]].astype(F32) * ex_ref[c, 0:L, hs[h]]).astype(BF16)
            o = av[c][h] + _dot_nt(qd, st_ref[h].astype(BF16))
            on = o * lax.rsqrt(jnp.mean(o * o, axis=-1, keepdims=True) + EPS) * og
            o_ref[rs[c], hs[h]] = on.astype(o_ref.dtype)
        for h in heads:
            kd = (k_ref[rs[c], hs[h]].astype(F32) * ex_ref[c, L:2 * L, hs[h]]).astype(BF16)
            st_ref[h] = st_ref[h] * ex_ref[c, L - 1:L, hs[h]] + _dot_tn(v_ref[rs[c], hs[h]], kd)


def _hgrn2_layer(x2, mod, layer, norm_g, lb, w_in, w_out, onorm_g, B, T):
    tm = ROW_TILE
    tt = T // tm
    N = B * T
    q, k, v, z, lf = pl.pallas_call(
        _hg_in_kernel,
        grid=(B, tt),
        in_specs=[_row_spec(tm, D_MODEL, tt), _const_spec((1, D_MODEL)), _mod_spec(layer),
                  _const_spec((D_MODEL, 4 * D_MODEL)), _const_spec((1, D_MODEL))],
        out_specs=[_row_spec(tm, D_MODEL, tt)] * 5,
        out_shape=[jax.ShapeDtypeStruct((N, D_MODEL), BF16)] * 4 + [jax.ShapeDtypeStruct((N, D_MODEL), F32)],
        compiler_params=_params(2),
        name=f"hgrn2_in_{layer}",
    )(x2, norm_g.reshape(1, D_MODEL), mod, w_in.astype(BF16), lb.reshape(1, D_MODEL))

    L = HG_CHUNK
    nc = HG_STEP_CHUNKS
    tc = T // (nc * L)
    cum, msk = _hg_tables(L)
    o = pl.pallas_call(
        functools.partial(_hg_scan_kernel, L=L),
        grid=(B, tc),
        in_specs=[_row_spec(nc * L, D_MODEL, tc)] * 4 + [_const_spec(cum.shape), _const_spec(msk.shape),
                                                          _const_spec((1, HG_DK))],
        out_specs=_row_spec(nc * L, D_MODEL, tc),
        out_shape=jax.ShapeDtypeStruct((N, D_MODEL), BF16),
        scratch_shapes=[pltpu.VMEM((HG_HEADS, HG_DK, HG_DK), F32),
                        pltpu.VMEM((nc, (2 + len(_hg_levels(L))) * L, D_MODEL), F32),
                        pltpu.VMEM((nc, L, D_MODEL), F32)],
        compiler_params=_params(2),
        name=f"hgrn2_scan_{layer}",
    )(q, k, v, lf, cum, msk, onorm_g.reshape(1, HG_DK))
    return _out_proj(o, z, x2, mod, layer, w_out, B, T)


def _sw_in_kernel(x_ref, g_ref, mod_ref, w_ref, pos_ref, invf_ref, qg_ref, kg_ref, seg_ref,
                  q_ref, kd_ref, vd_ref, z_ref):
    h = _prenorm(x_ref, g_ref, mod_ref)
    tm = h.shape[0]
    ang = invf_ref[...] * pos_ref[...].astype(F32)
    cos_f = jnp.cos(ang)
    sin_f = jnp.sin(ang)
    cs = jnp.concatenate([cos_f, sin_f, cos_f, sin_f], axis=0).T
    lane = lax.broadcasted_iota(jnp.int32, (tm, LANES), 1)
    first = (lane & (SW_DH // 2)) == 0
    cos_t = jnp.where(first, cs, pltpu.roll(cs, SW_DH // 2, 1))
    sin_a = jnp.where(first, -pltpu.roll(cs, LANES - SW_DH // 2, 1), 0.0)
    sin_b = jnp.where(first, 0.0, cs)
    low = lane < SW_DH
    seg = seg_ref[...]
    W2 = 2 * LANES

    xq = _dot(h, w_ref[:, 0:SW_QW])
    xk = _dot(h, w_ref[:, SW_QW:SW_QW + SW_KVW])
    xv = _dot(h, w_ref[:, SW_QW + SW_KVW:SW_QW + 2 * SW_KVW])
    z_ref[...] = _dot(h, w_ref[:, SW_QW + 2 * SW_KVW:]).astype(BF16)
    ms_q = [_dot((xq[:, c * W2:(c + 1) * W2] * xq[:, c * W2:(c + 1) * W2]).astype(BF16), seg)
            for c in range(SW_QW // W2)]
    ms_k = [_dot((xk[:, c * W2:(c + 1) * W2] * xk[:, c * W2:(c + 1) * W2]).astype(BF16), seg)
            for c in range(SW_KVW // W2)]

    def norm_rope(xg, ms, gain):
        xn = xg * lax.rsqrt(ms + EPS) * gain
        return (xn * cos_t + pltpu.roll(xn, LANES - SW_DH // 2, 1) * sin_a
                + pltpu.roll(xn, SW_DH // 2, 1) * sin_b)

    def group(x, ms, c):
        half = (c % 2) * LANES
        return x[:, c * LANES:(c + 1) * LANES], ms[c // 2][:, half:half + LANES]

    scale = SW_DH ** -0.5
    for c in range(SW_QW // LANES):
        xg, ms = group(xq, ms_q, c)
        q_ref[:, c * LANES:(c + 1) * LANES] = (norm_rope(xg, ms, qg_ref[...]) * scale).astype(BF16)
    for c in range(SW_KVW // LANES):
        xg, ms = group(xk, ms_k, c)
        kr = norm_rope(xg, ms, kg_ref[...])
        sw = pltpu.roll(kr, SW_DH, 1)
        for hh, (in_low, in_high) in enumerate(((kr, sw), (sw, kr))):
            base = (2 * c + hh) * 2 * LANES
            kd_ref[:, base:base + LANES] = jnp.where(low, in_low, 0.0).astype(BF16)
            kd_ref[:, base + LANES:base + 2 * LANES] = jnp.where(low, 0.0, in_high).astype(BF16)
        vg = xv[:, c * LANES:(c + 1) * LANES]
        sv = pltpu.roll(vg, SW_DH, 1)
        vd_ref[:, (2 * c) * LANES:(2 * c + 1) * LANES] = jnp.where(low, vg, sv).astype(BF16)
        vd_ref[:, (2 * c + 1) * LANES:(2 * c + 2) * LANES] = jnp.where(low, sv, vg).astype(BF16)


def _sw_attn_kernel(sink_ref, q_ref, kc_ref, kp_ref, vc_ref, vp_ref, o_ref):
    n = pl.program_id(1)
    BLK = SW_BLOCK
    qi = lax.broadcasted_iota(jnp.int32, (BLK, 2 * BLK), 0)
    kj = lax.broadcasted_iota(jnp.int32, (BLK, 2 * BLK), 1)
    allowed = ((kj < BLK) & (kj > qi) & (n > 0)) | ((kj >= BLK) & ((kj - BLK) <= qi))
    bias = jnp.where(allowed, 0.0, -jnp.inf)
    lane = lax.broadcasted_iota(jnp.int32, (BLK, LANES), 1)
    low = lane < SW_DH
    kv_heads = range(SW_KV_HEADS)
    n_pairs = SW_HEADS // SW_KV_HEADS // 2
    bias = jnp.concatenate([bias] * n_pairs, axis=0)
    row_pair = lax.broadcasted_iota(jnp.int32, (n_pairs * BLK, 1), 0) // BLK
    scores = {}
    for h in kv_heads:
        q2 = q_ref[:, h * n_pairs * LANES:(h + 1) * n_pairs * LANES]
        q2 = jnp.concatenate([q2[:, j * LANES:(j + 1) * LANES] for j in range(n_pairs)], axis=0)
        for half in range(2):
            ks = slice((2 * h + half) * LANES, (2 * h + half + 1) * LANES)
            kk = jnp.concatenate([kp_ref[:, ks], kc_ref[:, ks]], axis=0)
            scores[h, half] = _dot_nt(q2, kk) + bias
    probs, denom = {}, {}
    for h in kv_heads:
        for half in range(2):
            sink = sink_ref[2 * h * n_pairs + half]
            for j in range(1, n_pairs):
                sink = jnp.where(row_pair == j, sink_ref[2 * (h * n_pairs + j) + half], sink)
            s = scores[h, half]
            m = jnp.maximum(jnp.max(s, axis=-1, keepdims=True), sink)
            p = jnp.exp(s - m)
            denom[h, half] = jnp.sum(p, axis=-1, keepdims=True) + jnp.exp(sink - m)
            probs[h, half] = p.astype(BF16)
    pv = {}
    for h in kv_heads:
        hs = slice(h * LANES, (h + 1) * LANES)
        vv = jnp.concatenate([vp_ref[:, hs], vc_ref[:, hs]], axis=0)
        pv[h] = _dot(jnp.concatenate([probs[h, 0], probs[h, 1]], axis=0), vv)
    R = n_pairs * BLK
    for h in kv_heads:
        o0 = pv[h][:R] / denom[h, 0]
        o1 = pv[h][R:] / denom[h, 1]
        for j in range(n_pairs):
            c = h * n_pairs + j
            o_ref[:, c * LANES:(c + 1) * LANES] = jnp.where(
                low, o0[j * BLK:(j + 1) * BLK], o1[j * BLK:(j + 1) * BLK]).astype(o_ref.dtype)


def _swa_layer(x2, mod, layer, norm_g, positions, w_in, w_out, qn_g, kn_g, sinks, B, T):
    tm = ROW_TILE
    tt = T // tm
    N = B * T
    inv_freq = ROPE_THETA ** (-jnp.arange(0, SW_DH, 2, dtype=F32) / SW_DH)
    invf = inv_freq.reshape(SW_DH // 2, 1)
    seg = jnp.asarray(np.kron(np.eye(2 * LANES // SW_DH), np.full((SW_DH, SW_DH), 1.0 / SW_DH)), BF16)
    tile2 = lambda g: jnp.tile(g.astype(F32), LANES // SW_DH).reshape(1, LANES)
    q, kd, vd, z = pl.pallas_call(
        _sw_in_kernel,
        grid=(B, tt),
        in_specs=[_row_spec(tm, D_MODEL, tt), _const_spec((1, D_MODEL)), _mod_spec(layer),
                  _const_spec((D_MODEL, w_in.shape[1])),
                  pl.BlockSpec((None, 1, tm), lambda b, i: (b * tt + i, 0, 0)),
                  _const_spec((SW_DH // 2, 1)), _const_spec((1, LANES)), _const_spec((1, LANES)),
                  _const_spec((2 * LANES, 2 * LANES))],
        out_specs=[_row_spec(tm, SW_QW, tt), _row_spec(tm, 4 * SW_KVW, tt),
                   _row_spec(tm, 2 * SW_KVW, tt), _row_spec(tm, SW_QW, tt)],
        out_shape=[jax.ShapeDtypeStruct((N, SW_QW), BF16), jax.ShapeDtypeStruct((N, 4 * SW_KVW), BF16),
                   jax.ShapeDtypeStruct((N, 2 * SW_KVW), BF16), jax.ShapeDtypeStruct((N, SW_QW), BF16)],
        compiler_params=_params(2),
        name=f"swa_in_{layer}",
    )(x2, norm_g.reshape(1, D_MODEL), mod, w_in.astype(BF16), positions.reshape(N // tm, 1, tm),
      invf, tile2(qn_g), tile2(kn_g), seg)

    nb = T // SW_BLOCK
    cur = lambda b, n: (b * nb + n, 0)
    prev = lambda b, n: (b * nb + jnp.maximum(n - 1, 0), 0)
    kw, vw = 4 * SW_KVW, 2 * SW_KVW
    o = pl.pallas_call(
        _sw_attn_kernel,
        grid=(B, nb),
        in_specs=[pl.BlockSpec(memory_space=pltpu.SMEM),
                  pl.BlockSpec((SW_BLOCK, SW_QW), cur),
                  pl.BlockSpec((SW_BLOCK, kw), cur), pl.BlockSpec((SW_BLOCK, kw), prev),
                  pl.BlockSpec((SW_BLOCK, vw), cur), pl.BlockSpec((SW_BLOCK, vw), prev)],
        out_specs=pl.BlockSpec((SW_BLOCK, SW_QW), cur),
        out_shape=jax.ShapeDtypeStruct((N, SW_QW), BF16),
        compiler_params=_params(2),
        name=f"swa_attn_{layer}",
    )(sinks.astype(F32), q, kd, kd, vd, vd)
    return _out_proj(o, z, x2, mod, layer, w_out, B, T)


def _gd_in_kernel(x_ref, g_ref, mod_ref, w_ref, wab_hi_ref, wab_lo_ref, cw_ref, alog_ref, dtb_ref,
                  q_ref, k_ref, v_ref, z_ref, gb_ref, hist_ref):
    tm = x_ref.shape[0]
    H = SUBLANES

    @pl.when(pl.program_id(1) == 0)
    def _():
        hist_ref[...] = jnp.zeros_like(hist_ref)

    x = x_ref[...]
    b = pl.program_id(0)
    y = x * lax.rsqrt(jnp.mean(x * x, axis=-1, keepdims=True) + EPS) * g_ref[...]
    m = mod_ref[pl.ds(b, 1), :]
    hf = y * (1.0 + m[:, D_MODEL:2 * D_MODEL]) + m[:, :D_MODEL]
    h = hf.astype(BF16)
    h_lo = (hf - h.astype(F32)).astype(BF16)

    for s in range(GD_QKV // D_MODEL):
        cs = slice(s * D_MODEL, (s + 1) * D_MODEL)
        cur = _dot(h, w_ref[:, cs])
        taps = [cw_ref[GD_CONV - 1 - j:GD_CONV - j, cs] for j in range(GD_CONV)]
        grp = jnp.concatenate([hist_ref[:, cs], cur], axis=0).reshape(tm // H + 1, H, D_MODEL)
        sub = lax.broadcasted_iota(jnp.int32, (tm // H, H, D_MODEL), 1)
        conv = taps[0] * cur
        for j in range(1, GD_CONV):
            rot = pltpu.roll(grp, j, 1)
            xj = jnp.where(sub < j, rot[:-1], rot[1:]).reshape(tm, D_MODEL)
            conv = conv + taps[j] * xj
        hist_ref[:, cs] = cur[tm - H:tm]
        act = _silu(conv)
        if s < 2:
            for c in range(D_MODEL // GD_DK):
                a = act[:, c * GD_DK:(c + 1) * GD_DK]
                r = lax.rsqrt(jnp.sum(a * a, axis=-1, keepdims=True) + EPS)
                if s == 0:
                    q_ref[:, c * GD_DK:(c + 1) * GD_DK] = (a * r * GD_DK ** -0.5).astype(BF16)
                else:
                    k_ref[:, c * GD_DK:(c + 1) * GD_DK] = (a * r).astype(BF16)
        else:
            v_ref[:, (s - 2) * D_MODEL:(s - 1) * D_MODEL] = act.astype(BF16)
    z_ref[...] = _dot(h, w_ref[:, GD_QKV:GD_QKV + GD_VW]).astype(BF16)

    ab = _dot(h, wab_hi_ref[...]) + _dot(h, wab_lo_ref[...]) + _dot(h_lo, wab_hi_ref[...])
    xa = ab + dtb_ref[...]
    softplus = jnp.maximum(xa, 0.0) + jnp.log(1.0 + jnp.exp(-jnp.abs(xa)))
    gv = -jnp.exp(alog_ref[...]) * softplus
    lane = lax.broadcasted_iota(jnp.int32, (tm, LANES), 1)
    gb_ref[...] = jnp.where(lane < GD_V_HEADS, gv, _sigmoid(ab))


def _blockdiag(a, b):
    return jnp.concatenate([jnp.concatenate([a, jnp.zeros_like(b)], axis=1),
                            jnp.concatenate([jnp.zeros_like(a), b], axis=1)], axis=0)


def _pair_lhs(xp):
    hi, lo = _split2(xp)
    return jnp.concatenate([hi, lo], axis=1)


def _pair_rhs(yp, first):
    hi32 = yp.astype(BF16).astype(F32)
    lo32 = yp - hi32
    a_hi = jnp.where(first, hi32, 0.0).astype(BF16)
    b_hi = jnp.where(first, 0.0, hi32).astype(BF16)
    a_lo = jnp.where(first, lo32, 0.0).astype(BF16)
    b_lo = jnp.where(first, 0.0, lo32).astype(BF16)
    z = jnp.zeros_like(a_hi)
    return jnp.concatenate([jnp.concatenate([a_hi, b_hi, a_hi, b_hi], axis=0),
                            jnp.concatenate([a_lo, b_lo, z, z], axis=0)], axis=1)


def _pair_mm(lhs, rhs):
    o = _dot(lhs, rhs)
    return o[:, :LANES] + o[:, LANES:]


def _gd_scan_kernel(q_ref, k_ref, v_ref, gb_ref, tri_ref, og_ref, o_ref, st_ref):
    C = GD_CHUNK
    assert 2 * C == LANES and GD_V_HEADS == 2 * GD_QK_HEADS

    @pl.when(pl.program_id(1) == 0)
    def _():
        st_ref[...] = jnp.zeros_like(st_ref)

    ti = lax.broadcasted_iota(jnp.int32, (C, LANES), 0)
    lane = lax.broadcasted_iota(jnp.int32, (C, LANES), 1)
    tj = lane & (C - 1)
    first = lane < C
    incl = tj <= ti
    strict = tj < ti
    eye = (ti == tj).astype(F32)
    n_chunks = q_ref.shape[0] // C
    chunks = range(n_chunks)
    rs = [slice(c * C, (c + 1) * C) for c in chunks]
    pairs = range(GD_QK_HEADS)
    qs = [slice(p * GD_DK, (p + 1) * GD_DK) for p in pairs]
    items = [(c, p) for c in chunks for p in pairs]
    W = GD_DV
    gb = [gb_ref[rs[c], :] for c in chunks]
    d_all = [_dot_exact01(tri_ref[...], gb[c]) for c in chunks]

    def col(x, h):
        return x[:, h:h + 1]

    def both(x, p, off=0):
        return jnp.where(first, col(x, off + 2 * p), col(x, off + 2 * p + 1))

    kk, qk, dec, t_inv, pw = {}, {}, {}, {}, {}
    for c, p in items:
        k2 = jnp.concatenate([k_ref[rs[c], qs[p]], k_ref[rs[c], qs[p]]], axis=0)
        kq = _dot_nt(jnp.concatenate([k_ref[rs[c], qs[p]], q_ref[rs[c], qs[p]]], axis=0), k2)
        kk[c, p] = kq[:C]
        qk[c, p] = kq[C:]
    for c, p in items:
        d_row = jnp.sum(jnp.where(ti <= tj, both(gb[c], p), 0.0), axis=0, keepdims=True)
        dec[c, p] = jnp.exp(jnp.where(incl, both(d_all[c], p) - d_row, -jnp.inf))
        a = kk[c, p] * both(gb[c], p, GD_V_HEADS) * jnp.where(strict, dec[c, p], 0.0)
        t_inv[c, p] = eye - a
        pw[c, p] = a
    rhs = {it: _pair_rhs(pw[it], first) for it in items}
    for _ in range(int(math.log2(C)) - 1):
        pw = {it: _pair_mm(_pair_lhs(pw[it]), rhs[it]) for it in items}
        rhs = {it: _pair_rhs(pw[it], first) for it in items}
        t_inv = {it: t_inv[it] + _pair_mm(_pair_lhs(t_inv[it]), rhs[it]) for it in items}
    uw, ed, kf = {}, {}, {}
    for c, p in items:
        kf[c, p] = k_ref[rs[c], qs[p]].astype(F32)
        halves = []
        for h in (2 * p, 2 * p + 1):
            ed[c, h] = jnp.exp(col(d_all[c], h))
            beta = col(gb[c], GD_V_HEADS + h)
            v = v_ref[rs[c], h * W:(h + 1) * W].astype(F32)
            halves.append(jnp.concatenate([v * beta, kf[c, p] * beta * ed[c, h]], axis=1).astype(BF16))
        uw[c, p] = _dot(t_inv[c, p].astype(BF16), _blockdiag(halves[0], halves[1]))
    for c in chunks:
        s2 = [_blockdiag(st_ref[2 * p].astype(BF16), st_ref[2 * p + 1].astype(BF16)) for p in pairs]
        vn, q_s = [], []
        for p in pairs:
            u2 = jnp.concatenate([uw[c, p][:, 0:W], uw[c, p][:, 2 * W:3 * W]], axis=1)
            w2 = jnp.concatenate([uw[c, p][:, W:2 * W], uw[c, p][:, 3 * W:4 * W]], axis=1).astype(BF16)
            q = q_ref[rs[c], qs[p]].astype(F32)
            qd = jnp.concatenate([q * ed[c, 2 * p], q * ed[c, 2 * p + 1]], axis=1).astype(BF16)
            ws_qs = _dot(jnp.concatenate([w2, qd], axis=0), s2[p])
            vn.append((u2 - ws_qs[:C]).astype(BF16))
            q_s.append(ws_qs[C:])
        for p in pairs:
            o = q_s[p] + _dot((qk[c, p] * dec[c, p]).astype(BF16), _blockdiag(vn[p][:, :W], vn[p][:, W:]))
            for i in range(2):
                oh = o[:, i * W:(i + 1) * W]
                on = oh * lax.rsqrt(jnp.mean(oh * oh, axis=-1, keepdims=True) + EPS) * og_ref[...]
                o_ref[rs[c], (2 * p + i) * W:(2 * p + i + 1) * W] = on.astype(o_ref.dtype)
        for p in pairs:
            for i in range(2):
                h = 2 * p + i
                d_last = col(d_all[c], h)[C - 1:C, :]
                kd = (kf[c, p] * jnp.exp(d_last - col(d_all[c], h))).astype(BF16)
                st_ref[h] = st_ref[h] * jnp.exp(d_last) + _dot_tn(kd, vn[p][:, i * W:(i + 1) * W])


def _gdn_layer(x2, mod, layer, norm_g, w_in, w_out, conv_w, a_log, dt_bias, onorm_g, B, T):
    tm = GD_ROW_TILE
    tt = T // tm
    N = B * T
    w_main = w_in[:, :GD_QKV + GD_VW].astype(BF16)
    w_ab = jnp.zeros((D_MODEL, LANES), F32).at[:, :2 * GD_V_HEADS].set(w_in[:, GD_QKV + GD_VW:])
    wab_hi = w_ab.astype(BF16)
    wab_lo = (w_ab - wab_hi.astype(F32)).astype(BF16)
    pad = lambda a: jnp.zeros((1, LANES), F32).at[0, :GD_V_HEADS].set(a.astype(F32))
    q, k, v, z, gb = pl.pallas_call(
        _gd_in_kernel,
        grid=(B, tt),
        in_specs=[_row_spec(tm, D_MODEL, tt), _const_spec((1, D_MODEL)), _mod_spec(layer),
                  _const_spec((D_MODEL, GD_QKV + GD_VW)), _const_spec((D_MODEL, LANES)),
                  _const_spec((D_MODEL, LANES)), _const_spec((GD_CONV, GD_QKV)),
                  _const_spec((1, LANES)), _const_spec((1, LANES))],
        out_specs=[_row_spec(tm, GD_QKW, tt), _row_spec(tm, GD_QKW, tt), _row_spec(tm, GD_VW, tt),
                   _row_spec(tm, GD_VW, tt), _row_spec(tm, LANES, tt)],
        out_shape=[jax.ShapeDtypeStruct((N, GD_QKW), BF16), jax.ShapeDtypeStruct((N, GD_QKW), BF16),
                   jax.ShapeDtypeStruct((N, GD_VW), BF16), jax.ShapeDtypeStruct((N, GD_VW), BF16),
                   jax.ShapeDtypeStruct((N, LANES), F32)],
        scratch_shapes=[pltpu.VMEM((SUBLANES, GD_QKV), F32)],
        compiler_params=_params(2),
        name=f"gdn_in_{layer}",
    )(x2, norm_g.reshape(1, D_MODEL), mod, w_main, wab_hi, wab_lo, conv_w.astype(F32),
      pad(a_log), pad(dt_bias))

    C = GD_CHUNK
    rows = GD_STEP_CHUNKS * C
    tc = T // rows
    tri = jnp.asarray(np.tril(np.ones((C, C), np.float32)), BF16)
    o = pl.pallas_call(
        _gd_scan_kernel,
        grid=(B, tc),
        in_specs=[_row_spec(rows, GD_QKW, tc), _row_spec(rows, GD_QKW, tc), _row_spec(rows, GD_VW, tc),
                  _row_spec(rows, LANES, tc), _const_spec((C, C)), _const_spec((1, GD_DV))],
        out_specs=_row_spec(rows, GD_VW, tc),
        out_shape=jax.ShapeDtypeStruct((N, GD_VW), BF16),
        scratch_shapes=[pltpu.VMEM((GD_V_HEADS, GD_DK, GD_DV), F32)],
        compiler_params=_params(2),
        name=f"gdn_scan_{layer}",
    )(q, k, v, gb, tri, onorm_g.reshape(1, GD_DV))
    return _out_proj(o, z, x2, mod, layer, w_out, B, T)


def kernel(x, c, positions, hgrn_lb, ada_w, ada_b, norm_g, hg_in_w, hg_out_w, hg_onorm, sw_in_w, sw_out_w,
           sw_qnorm, sw_knorm, sw_sinks, gd_in_w, gd_out_w, gd_conv_w, gd_a_log, gd_dt_bias, gd_onorm):
    B, T, _ = x.shape
    lb_all = jnp.cumsum(jax.nn.softmax(hgrn_lb.astype(F32), axis=0), axis=0)
    lb_all = lb_all - lb_all[0:1]
    mod = _ada_mod(c, ada_w, ada_b)
    x2 = x.reshape(B * T, D_MODEL)
    for i in range(DEPTH):
        j, kind = divmod(i, 3)
        if kind == 0:
            x2 = _hgrn2_layer(x2, mod, i, norm_g[i], lb_all[i], hg_in_w[j], hg_out_w[j], hg_onorm[j], B, T)
        elif kind == 1:
            x2 = _swa_layer(x2, mod, i, norm_g[i], positions, sw_in_w[j], sw_out_w[j], sw_qnorm[j],
                            sw_knorm[j], sw_sinks[j], B, T)
        else:
            x2 = _gdn_layer(x2, mod, i, norm_g[i], gd_in_w[j], gd_out_w[j], gd_conv_w[j], gd_a_log[j],
                            gd_dt_bias[j], gd_onorm[j], B, T)
    return x2.reshape(B, T, D_MODEL)
```

```python
import functools
import math

import numpy as np
import jax
import jax.numpy as jnp
from jax import lax
from jax.experimental import pallas as pl
from jax.experimental.pallas import tpu as pltpu

F32 = jnp.float32
BF16 = jnp.bfloat16

D_MODEL = 1024
DEPTH = 4
EPS = 1e-6
LANES = 128
SUBLANES = 8
VMEM_LIMIT = 56 * 2**20

HG_HEADS = 8
HG_DK = 128
HG_CHUNK = 64
HG_STEP_CHUNKS = 4

SW_HEADS = 16
SW_KV_HEADS = 4
SW_DH = 64
SW_BLOCK = 128
SW_QW = SW_HEADS * SW_DH
SW_KVW = SW_KV_HEADS * SW_DH
ROPE_THETA = 10000.0

GD_QK_HEADS = 8
GD_V_HEADS = 16
GD_DK = 128
GD_DV = 128
GD_CONV = 4
GD_CHUNK = 64
GD_STEP_CHUNKS = 2
GD_QKW = GD_QK_HEADS * GD_DK
GD_VW = GD_V_HEADS * GD_DV
GD_QKV = 2 * GD_QKW + GD_VW

ROW_TILE = 512
GD_ROW_TILE = 512
GD_CONV_ROWS = 64


def _params(n_axes):
    return pltpu.CompilerParams(dimension_semantics=("arbitrary",) * n_axes,
                                vmem_limit_bytes=VMEM_LIMIT)


def _dot(a, b):
    return jnp.dot(a, b, preferred_element_type=F32)


def _dot_nt(a, b):
    return lax.dot_general(a, b, (((1,), (1,)), ((), ())), preferred_element_type=F32)


def _dot_tn(a, b):
    return lax.dot_general(a, b, (((0,), (0,)), ((), ())), preferred_element_type=F32)


def _split2(x):
    hi = x.astype(BF16)
    lo = (x - hi.astype(F32)).astype(BF16)
    return hi, lo


def _dot_exact01(m01, x):
    hi, lo = _split2(x)
    return _dot(m01, hi) + _dot(m01, lo)


def _sigmoid(x):
    return 0.5 * jnp.tanh(0.5 * x) + 0.5


def _silu(x):
    h = 0.5 * x
    return h * jnp.tanh(h) + h


def _ada_kernel(c_ref, w_ref, b_ref, o_ref):
    o_ref[...] = jnp.dot(c_ref[...], w_ref[...], precision=lax.Precision.HIGHEST,
                         preferred_element_type=F32) + b_ref[...]


def _ada_mod(c, ada_w, ada_b):
    B = c.shape[0]
    depth = ada_w.shape[0]
    c8 = jnp.zeros((SUBLANES, D_MODEL), F32).at[:B].set(c)
    return pl.pallas_call(
        _ada_kernel,
        grid=(depth, 3),
        in_specs=[pl.BlockSpec((SUBLANES, D_MODEL), lambda i, j: (0, 0)),
                  pl.BlockSpec((None, D_MODEL, D_MODEL), lambda i, j: (i, 0, j)),
                  pl.BlockSpec((None, 1, D_MODEL), lambda i, j: (i, 0, j))],
        out_specs=pl.BlockSpec((None, SUBLANES, D_MODEL), lambda i, j: (i, 0, j)),
        out_shape=jax.ShapeDtypeStruct((depth, SUBLANES, 3 * D_MODEL), F32),
        compiler_params=_params(2),
        name="ada_mod",
    )(c8, ada_w, ada_b.reshape(depth, 1, 3 * D_MODEL))


def _prenorm(x_ref, g_ref, mod_ref):
    b = pl.program_id(0)
    x = x_ref[...]
    y = x * lax.rsqrt(jnp.mean(x * x, axis=-1, keepdims=True) + EPS) * g_ref[...]
    m = mod_ref[pl.ds(b, 1), :]
    shift = m[:, :D_MODEL]
    scale = m[:, D_MODEL:2 * D_MODEL]
    return (y * (1.0 + scale) + shift).astype(BF16)


def _row_spec(tm, width, t_tiles):
    return pl.BlockSpec((tm, width), lambda b, i: (b * t_tiles + i, 0))


def _const_spec(shape):
    return pl.BlockSpec(shape, lambda b, i: (0,) * len(shape))


def _mod_spec(layer):
    return pl.BlockSpec((None, SUBLANES, 3 * D_MODEL), lambda b, i: (layer, 0, 0))


def _out_kernel(o_ref, z_ref, x_ref, mod_ref, w_ref, out_ref):
    b = pl.program_id(0)
    z = z_ref[...].astype(F32)
    y = (o_ref[...].astype(F32) * _silu(z)).astype(BF16)
    acc = _dot(y, w_ref[...])
    gate = mod_ref[pl.ds(b, 1), :][:, 2 * D_MODEL:]
    out_ref[...] = x_ref[...] + gate * acc


def _out_proj(o, z, x2, mod, layer, w_out, B, T):
    tm = ROW_TILE
    tt = T // tm
    wo = w_out.shape[0]
    return pl.pallas_call(
        _out_kernel,
        grid=(B, tt),
        in_specs=[_row_spec(tm, wo, tt), _row_spec(tm, wo, tt), _row_spec(tm, D_MODEL, tt),
                  _mod_spec(layer), _const_spec((wo, D_MODEL))],
        out_specs=_row_spec(tm, D_MODEL, tt),
        out_shape=jax.ShapeDtypeStruct((B * T, D_MODEL), F32),
        compiler_params=_params(2),
        name=f"out_proj_{layer}",
    )(o, z, x2, mod, w_out.astype(BF16))


def _hg_in_kernel(x_ref, g_ref, mod_ref, w_ref, lb_ref, q_ref, k_ref, v_ref, z_ref, lf_ref):
    h = _prenorm(x_ref, g_ref, mod_ref)
    lb = lb_ref[...]
    qp = _dot(h, w_ref[:, 0:D_MODEL])
    q_ref[...] = _silu(qp).astype(BF16)
    fp = _dot(h, w_ref[:, D_MODEL:2 * D_MODEL])
    th = 0.5 * jnp.tanh(0.5 * fp)
    lf_ref[...] = jnp.log(lb + (1.0 - lb) * (0.5 + th))
    k_ref[...] = ((1.0 - lb) * (0.5 - th)).astype(BF16)
    v_ref[...] = _dot(h, w_ref[:, 2 * D_MODEL:3 * D_MODEL]).astype(BF16)
    z_ref[...] = _dot(h, w_ref[:, 3 * D_MODEL:4 * D_MODEL]).astype(BF16)


def _hg_levels(L):
    m, out = L // 2, []
    while m >= 1:
        out.append(m)
        m //= 2
    return out


def _hg_mxu_levels(L):
    return [m for m in _hg_levels(L) if 1 < m < SUBLANES]


def _hg_tables(L):
    t = np.arange(L)[:, None]
    u = np.arange(L)[None, :]
    masks = [np.eye(L, dtype=bool)]
    for m in _hg_levels(L):
        r = (t // (2 * m)) * (2 * m) + m
        ru = (u // (2 * m)) * (2 * m) + m
        masks.append((t // (2 * m) == u // (2 * m)) & (t >= r) & (u < ru))
    secs = [u <= t]
    for m in _hg_mxu_levels(L):
        r = (t // (2 * m)) * (2 * m) + m
        secs.append(np.where(t >= r, (u >= r) & (u <= t), (u > t) & (u < r)))
    cum = np.concatenate(secs, axis=0).astype(np.float32)
    masks = np.stack(masks).astype(np.float32)
    return jnp.asarray(cum, BF16), jnp.asarray(np.concatenate([masks, masks], axis=2))


def _hg_exponents(lf, sums, ex_ref, b_ref, L, levels):
    G = SUBLANES
    width = lf.shape[1]
    mxu_levels = _hg_mxu_levels(L)
    b_ref[...] = sums[0:L]
    odd = (lax.broadcasted_iota(jnp.int32, (L, width), 0) & 1) != 0
    for li, m in enumerate(levels):
        dst = slice((2 + li) * L, (3 + li) * L)
        if m in mxu_levels:
            k = 1 + mxu_levels.index(m)
            ex_ref[dst, :] = jnp.exp(sums[k * L:(k + 1) * L])
        elif m == 1:
            ex_ref[dst, :] = jnp.exp(jnp.where(odd, lf, 0.0))
    rows = {}

    def brow(r):
        if r not in rows:
            rows[r] = jnp.broadcast_to(b_ref[r:r + 1, :], (G, width))
        return rows[r]

    for g in range(L // G):
        base = g * G
        b = b_ref[base:base + G, :]
        ex_ref[base:base + G, :] = jnp.exp(b)
        ex_ref[L + base:L + base + G, :] = jnp.exp(brow(L - 1) - b)
        for li, m in enumerate(levels):
            if m >= G:
                ref_row = (base // (2 * m)) * (2 * m) + m - 1
                d = b - brow(ref_row)
                ex_ref[(2 + li) * L + base:(2 + li) * L + base + G, :] = jnp.exp(d if base % (2 * m) >= m else -d)


def _hg_scan_kernel(q_ref, k_ref, v_ref, lf_ref, cum_ref, msk_ref, og_ref, o_ref, st_ref, ex_ref, b_ref, *, L):
    levels = _hg_levels(L)
    n_chunks = q_ref.shape[0] // L

    @pl.when(pl.program_id(1) == 0)
    def _():
        st_ref[...] = jnp.zeros_like(st_ref)

    W2 = 2 * HG_DK
    row = lax.broadcasted_iota(jnp.int32, (L, W2), 0)
    og = og_ref[...]
    heads = range(HG_HEADS)
    pairs = range(HG_HEADS // 2)
    chunks = range(n_chunks)
    hs = [slice(h * HG_DK, (h + 1) * HG_DK) for h in heads]
    ps = [slice(p * W2, (p + 1) * W2) for p in pairs]
    rs = [slice(c * L, (c + 1) * L) for c in chunks]

    def bd(x2):
        return _blockdiag(x2[:, :HG_DK], x2[:, HG_DK:])

    for c in chunks:
        sums = _dot_exact01(cum_ref[...], lf_ref[rs[c], :])
        _hg_exponents(lf_ref[rs[c], :], sums, ex_ref.at[c], b_ref.at[c], L, levels)
    a = [[msk_ref[0] * _dot_nt(q_ref[rs[c], ps[p]], bd(k_ref[rs[c], ps[p]])) for p in pairs] for c in chunks]
    for li, m in enumerate(levels):
        upper = (row & m) != 0
        for c in chunks:
            for p in pairs:
                e = ex_ref[c, (2 + li) * L:(3 + li) * L, ps[p]]
                x = jnp.where(upper, q_ref[rs[c], ps[p]].astype(F32), k_ref[rs[c], ps[p]].astype(F32))
                xm = (x * e).astype(BF16)
                a[c][p] = a[c][p] + msk_ref[li + 1] * _dot_nt(xm, bd(xm))
    av = [[_dot(a[c][p].astype(BF16), bd(v_ref[rs[c], ps[p]])) for p in pairs] for c in chunks]
    for c in chunks:
        for p in pairs:
            qd = (q_ref[rs[c], ps[p]].astype(F32) * ex_ref[c, 0:L, ps[p]]).astype(BF16)
            s2 = _blockdiag(st_ref[2 * p].astype(BF16), st_ref[2 * p + 1].astype(BF16))
            o = av[c][p] + _dot_nt(qd, s2)
            for i in range(2):
                oh = o[:, i * HG_DK:(i + 1) * HG_DK]
                on = oh * lax.rsqrt(jnp.mean(oh * oh, axis=-1, keepdims=True) + EPS) * og
                o_ref[rs[c], hs[2 * p + i]] = on.astype(o_ref.dtype)
        for h in heads:
            kd = (k_ref[rs[c], hs[h]].astype(F32) * ex_ref[c, L:2 * L, hs[h]]).astype(BF16)
            st_ref[h] = st_ref[h] * ex_ref[c, L - 1:L, hs[h]] + _dot_tn(v_ref[rs[c], hs[h]], kd)


def _hgrn2_layer(x2, mod, layer, norm_g, lb, w_in, w_out, onorm_g, B, T):
    tm = ROW_TILE
    tt = T // tm
    N = B * T
    q, k, v, z, lf = pl.pallas_call(
        _hg_in_kernel,
        grid=(B, tt),
        in_specs=[_row_spec(tm, D_MODEL, tt), _const_spec((1, D_MODEL)), _mod_spec(layer),
                  _const_spec((D_MODEL, 4 * D_MODEL)), _const_spec((1, D_MODEL))],
        out_specs=[_row_spec(tm, D_MODEL, tt)] * 5,
        out_shape=[jax.ShapeDtypeStruct((N, D_MODEL), BF16)] * 4 + [jax.ShapeDtypeStruct((N, D_MODEL), F32)],
        compiler_params=_params(2),
        name=f"hgrn2_in_{layer}",
    )(x2, norm_g.reshape(1, D_MODEL), mod, w_in.astype(BF16), lb.reshape(1, D_MODEL))

    L = HG_CHUNK
    nc = HG_STEP_CHUNKS
    tc = T // (nc * L)
    cum, msk = _hg_tables(L)
    o = pl.pallas_call(
        functools.partial(_hg_scan_kernel, L=L),
        grid=(B, tc),
        in_specs=[_row_spec(nc * L, D_MODEL, tc)] * 4 + [_const_spec(cum.shape), _const_spec(msk.shape),
                                                          _const_spec((1, HG_DK))],
        out_specs=_row_spec(nc * L, D_MODEL, tc),
        out_shape=jax.ShapeDtypeStruct((N, D_MODEL), BF16),
        scratch_shapes=[pltpu.VMEM((HG_HEADS, HG_DK, HG_DK), F32),
                        pltpu.VMEM((nc, (2 + len(_hg_levels(L))) * L, D_MODEL), F32),
                        pltpu.VMEM((nc, L, D_MODEL), F32)],
        compiler_params=_params(2),
        name=f"hgrn2_scan_{layer}",
    )(q, k, v, lf, cum, msk, onorm_g.reshape(1, HG_DK))
    return _out_proj(o, z, x2, mod, layer, w_out, B, T)


def _sw_in_kernel(x_ref, g_ref, mod_ref, w_ref, pos_ref, invf_ref, qg_ref, kg_ref, seg_ref,
                  q_ref, kd_ref, vd_ref, z_ref):
    h = _prenorm(x_ref, g_ref, mod_ref)
    tm = h.shape[0]
    ang = invf_ref[...] * pos_ref[...].astype(F32)
    cos_f = jnp.cos(ang)
    sin_f = jnp.sin(ang)
    cs = jnp.concatenate([cos_f, sin_f, cos_f, sin_f], axis=0).T
    lane = lax.broadcasted_iota(jnp.int32, (tm, LANES), 1)
    first = (lane & (SW_DH // 2)) == 0
    cos_t = jnp.where(first, cs, pltpu.roll(cs, SW_DH // 2, 1))
    sin_a = jnp.where(first, -pltpu.roll(cs, LANES - SW_DH // 2, 1), 0.0)
    sin_b = jnp.where(first, 0.0, cs)
    low = lane < SW_DH
    seg = seg_ref[...]
    W2 = 2 * LANES

    xq = _dot(h, w_ref[:, 0:SW_QW])
    xk = _dot(h, w_ref[:, SW_QW:SW_QW + SW_KVW])
    xv = _dot(h, w_ref[:, SW_QW + SW_KVW:SW_QW + 2 * SW_KVW])
    z_ref[...] = _dot(h, w_ref[:, SW_QW + 2 * SW_KVW:]).astype(BF16)
    ms_q = [_dot((xq[:, c * W2:(c + 1) * W2] * xq[:, c * W2:(c + 1) * W2]).astype(BF16), seg)
            for c in range(SW_QW // W2)]
    ms_k = [_dot((xk[:, c * W2:(c + 1) * W2] * xk[:, c * W2:(c + 1) * W2]).astype(BF16), seg)
            for c in range(SW_KVW // W2)]

    def norm_rope(xg, ms, gain):
        xn = xg * lax.rsqrt(ms + EPS) * gain
        return (xn * cos_t + pltpu.roll(xn, LANES - SW_DH // 2, 1) * sin_a
                + pltpu.roll(xn, SW_DH // 2, 1) * sin_b)

    def group(x, ms, c):
        half = (c % 2) * LANES
        return x[:, c * LANES:(c + 1) * LANES], ms[c // 2][:, half:half + LANES]

    scale = SW_DH ** -0.5
    for c in range(SW_QW // LANES):
        xg, ms = group(xq, ms_q, c)
        q_ref[:, c * LANES:(c + 1) * LANES] = (norm_rope(xg, ms, qg_ref[...]) * scale).astype(BF16)
    for c in range(SW_KVW // LANES):
        xg, ms = group(xk, ms_k, c)
        kr = norm_rope(xg, ms, kg_ref[...])
        sw = pltpu.roll(kr, SW_DH, 1)
        for hh, (in_low, in_high) in enumerate(((kr, sw), (sw, kr))):
            base = (2 * c + hh) * 2 * LANES
            kd_ref[:, base:base + LANES] = jnp.where(low, in_low, 0.0).astype(BF16)
            kd_ref[:, base + LANES:base + 2 * LANES] = jnp.where(low, 0.0, in_high).astype(BF16)
        vg = xv[:, c * LANES:(c + 1) * LANES]
        sv = pltpu.roll(vg, SW_DH, 1)
        vd_ref[:, (2 * c) * LANES:(2 * c + 1) * LANES] = jnp.where(low, vg, sv).astype(BF16)
        vd_ref[:, (2 * c + 1) * LANES:(2 * c + 2) * LANES] = jnp.where(low, sv, vg).astype(BF16)


def _sw_attn_kernel(sink_ref, q_ref, kc_ref, kp_ref, vc_ref, vp_ref, o_ref):
    n = pl.program_id(1)
    BLK = SW_BLOCK
    qi = lax.broadcasted_iota(jnp.int32, (BLK, 2 * BLK), 0)
    kj = lax.broadcasted_iota(jnp.int32, (BLK, 2 * BLK), 1)
    allowed = ((kj < BLK) & (kj > qi) & (n > 0)) | ((kj >= BLK) & ((kj - BLK) <= qi))
    bias = jnp.where(allowed, 0.0, -jnp.inf)
    lane = lax.broadcasted_iota(jnp.int32, (BLK, LANES), 1)
    low = lane < SW_DH
    kv_heads = range(SW_KV_HEADS)
    n_pairs = SW_HEADS // SW_KV_HEADS // 2
    bias = jnp.concatenate([bias] * n_pairs, axis=0)
    row_pair = lax.broadcasted_iota(jnp.int32, (n_pairs * BLK, 1), 0) // BLK
    scores = {}
    for h in kv_heads:
        q2 = q_ref[:, h * n_pairs * LANES:(h + 1) * n_pairs * LANES]
        q2 = jnp.concatenate([q2[:, j * LANES:(j + 1) * LANES] for j in range(n_pairs)], axis=0)
        for half in range(2):
            ks = slice((2 * h + half) * LANES, (2 * h + half + 1) * LANES)
            kk = jnp.concatenate([kp_ref[:, ks], kc_ref[:, ks]], axis=0)
            scores[h, half] = _dot_nt(q2, kk) + bias
    probs, denom = {}, {}
    for h in kv_heads:
        for half in range(2):
            sink = sink_ref[2 * h * n_pairs + half]
            for j in range(1, n_pairs):
                sink = jnp.where(row_pair == j, sink_ref[2 * (h * n_pairs + j) + half], sink)
            s = scores[h, half]
            m = jnp.maximum(jnp.max(s, axis=-1, keepdims=True), sink)
            p = jnp.exp(s - m)
            denom[h, half] = jnp.sum(p, axis=-1, keepdims=True) + jnp.exp(sink - m)
            probs[h, half] = p.astype(BF16)
    pv = {}
    for h in kv_heads:
        hs = slice(h * LANES, (h + 1) * LANES)
        vv = jnp.concatenate([vp_ref[:, hs], vc_ref[:, hs]], axis=0)
        pv[h] = _dot(jnp.concatenate([probs[h, 0], probs[h, 1]], axis=0), vv)
    R = n_pairs * BLK
    for h in kv_heads:
        o0 = pv[h][:R] / denom[h, 0]
        o1 = pv[h][R:] / denom[h, 1]
        for j in range(n_pairs):
            c = h * n_pairs + j
            o_ref[:, c * LANES:(c + 1) * LANES] = jnp.where(
                low, o0[j * BLK:(j + 1) * BLK], o1[j * BLK:(j + 1) * BLK]).astype(o_ref.dtype)


def _swa_layer(x2, mod, layer, norm_g, positions, w_in, w_out, qn_g, kn_g, sinks, B, T):
    tm = ROW_TILE
    tt = T // tm
    N = B * T
    inv_freq = ROPE_THETA ** (-jnp.arange(0, SW_DH, 2, dtype=F32) / SW_DH)
    invf = inv_freq.reshape(SW_DH // 2, 1)
    seg = jnp.asarray(np.kron(np.eye(2 * LANES // SW_DH), np.full((SW_DH, SW_DH), 1.0 / SW_DH)), BF16)
    tile2 = lambda g: jnp.tile(g.astype(F32), LANES // SW_DH).reshape(1, LANES)
    q, kd, vd, z = pl.pallas_call(
        _sw_in_kernel,
        grid=(B, tt),
        in_specs=[_row_spec(tm, D_MODEL, tt), _const_spec((1, D_MODEL)), _mod_spec(layer),
                  _const_spec((D_MODEL, w_in.shape[1])),
                  pl.BlockSpec((None, 1, tm), lambda b, i: (b * tt + i, 0, 0)),
                  _const_spec((SW_DH // 2, 1)), _const_spec((1, LANES)), _const_spec((1, LANES)),
                  _const_spec((2 * LANES, 2 * LANES))],
        out_specs=[_row_spec(tm, SW_QW, tt), _row_spec(tm, 4 * SW_KVW, tt),
                   _row_spec(tm, 2 * SW_KVW, tt), _row_spec(tm, SW_QW, tt)],
        out_shape=[jax.ShapeDtypeStruct((N, SW_QW), BF16), jax.ShapeDtypeStruct((N, 4 * SW_KVW), BF16),
                   jax.ShapeDtypeStruct((N, 2 * SW_KVW), BF16), jax.ShapeDtypeStruct((N, SW_QW), BF16)],
        compiler_params=_params(2),
        name=f"swa_in_{layer}",
    )(x2, norm_g.reshape(1, D_MODEL), mod, w_in.astype(BF16), positions.reshape(N // tm, 1, tm),
      invf, tile2(qn_g), tile2(kn_g), seg)

    nb = T // SW_BLOCK
    cur = lambda b, n: (b * nb + n, 0)
    prev = lambda b, n: (b * nb + jnp.maximum(n - 1, 0), 0)
    kw, vw = 4 * SW_KVW, 2 * SW_KVW
    o = pl.pallas_call(
        _sw_attn_kernel,
        grid=(B, nb),
        in_specs=[pl.BlockSpec(memory_space=pltpu.SMEM),
                  pl.BlockSpec((SW_BLOCK, SW_QW), cur),
                  pl.BlockSpec((SW_BLOCK, kw), cur), pl.BlockSpec((SW_BLOCK, kw), prev),
                  pl.BlockSpec((SW_BLOCK, vw), cur), pl.BlockSpec((SW_BLOCK, vw), prev)],
        out_specs=pl.BlockSpec((SW_BLOCK, SW_QW), cur),
        out_shape=jax.ShapeDtypeStruct((N, SW_QW), BF16),
        compiler_params=_params(2),
        name=f"swa_attn_{layer}",
    )(sinks.astype(F32), q, kd, kd, vd, vd)
    return _out_proj(o, z, x2, mod, layer, w_out, B, T)


def _gd_in_kernel(x_ref, g_ref, mod_ref, w_ref, wab_hi_ref, wab_lo_ref, cw_ref, alog_ref, dtb_ref,
                  q_ref, k_ref, v_ref, z_ref, gb_ref, hist_ref, buf_ref):
    tm = x_ref.shape[0]
    H = SUBLANES

    @pl.when(pl.program_id(1) == 0)
    def _():
        hist_ref[...] = jnp.zeros_like(hist_ref)

    x = x_ref[...]
    b = pl.program_id(0)
    y = x * lax.rsqrt(jnp.mean(x * x, axis=-1, keepdims=True) + EPS) * g_ref[...]
    m = mod_ref[pl.ds(b, 1), :]
    hf = y * (1.0 + m[:, D_MODEL:2 * D_MODEL]) + m[:, :D_MODEL]
    h = hf.astype(BF16)
    h_lo = (hf - h.astype(F32)).astype(BF16)

    R = GD_CONV_ROWS
    sub = lax.broadcasted_iota(jnp.int32, (R // H, H, D_MODEL), 1)
    for s in range(GD_QKV // D_MODEL):
        cs = slice(s * D_MODEL, (s + 1) * D_MODEL)
        buf_ref[0:H, :] = hist_ref[:, cs]
        buf_ref[H:H + tm, :] = _dot(h, w_ref[:, cs])
        taps = [cw_ref[GD_CONV - 1 - j:GD_CONV - j, cs] for j in range(GD_CONV)]
        for r0 in range(0, tm, R):
            rows = slice(r0, r0 + R)
            grp = buf_ref[r0:r0 + R + H, :].reshape(R // H + 1, H, D_MODEL)
            conv = taps[0] * buf_ref[H + r0:H + r0 + R, :]
            for j in range(1, GD_CONV):
                rot = pltpu.roll(grp, j, 1)
                conv = conv + taps[j] * jnp.where(sub < j, rot[:-1], rot[1:]).reshape(R, D_MODEL)
            act = _silu(conv)
            if s < 2:
                for c in range(D_MODEL // GD_DK):
                    a = act[:, c * GD_DK:(c + 1) * GD_DK]
                    r = lax.rsqrt(jnp.sum(a * a, axis=-1, keepdims=True) + EPS)
                    if s == 0:
                        q_ref[rows, c * GD_DK:(c + 1) * GD_DK] = (a * (r * GD_DK ** -0.5)).astype(BF16)
                    else:
                        k_ref[rows, c * GD_DK:(c + 1) * GD_DK] = (a * r).astype(BF16)
            else:
                v_ref[rows, (s - 2) * D_MODEL:(s - 1) * D_MODEL] = act.astype(BF16)
        hist_ref[:, cs] = buf_ref[tm:tm + H, :]
    z_ref[...] = _dot(h, w_ref[:, GD_QKV:GD_QKV + GD_VW]).astype(BF16)

    ab = _dot(h, wab_hi_ref[...]) + _dot(h, wab_lo_ref[...]) + _dot(h_lo, wab_hi_ref[...])
    xa = ab + dtb_ref[...]
    softplus = jnp.maximum(xa, 0.0) + jnp.log(1.0 + jnp.exp(-jnp.abs(xa)))
    gv = -jnp.exp(alog_ref[...]) * softplus
    lane = lax.broadcasted_iota(jnp.int32, (tm, LANES), 1)
    gb_ref[...] = jnp.where(lane < GD_V_HEADS, gv, _sigmoid(ab))


def _blockdiag(a, b):
    return jnp.concatenate([jnp.concatenate([a, jnp.zeros_like(b)], axis=1),
                            jnp.concatenate([jnp.zeros_like(a), b], axis=1)], axis=0)


def _pair_lhs(xp):
    hi, lo = _split2(xp)
    return jnp.concatenate([hi, lo], axis=1)


def _pair_rhs(yp, first):
    hi32 = yp.astype(BF16).astype(F32)
    lo32 = yp - hi32
    a_hi = jnp.where(first, hi32, 0.0).astype(BF16)
    b_hi = jnp.where(first, 0.0, hi32).astype(BF16)
    a_lo = jnp.where(first, lo32, 0.0).astype(BF16)
    b_lo = jnp.where(first, 0.0, lo32).astype(BF16)
    z = jnp.zeros_like(a_hi)
    return jnp.concatenate([jnp.concatenate([a_hi, b_hi, a_hi, b_hi], axis=0),
                            jnp.concatenate([a_lo, b_lo, z, z], axis=0)], axis=1)


def _pair_mm(lhs, rhs):
    o = _dot(lhs, rhs)
    return o[:, :LANES] + o[:, LANES:]


def _gd_scan_kernel(q_ref, k_ref, v_ref, gb_ref, tri_ref, og_ref, o_ref, st_ref):
    C = GD_CHUNK
    assert 2 * C == LANES and GD_V_HEADS == 2 * GD_QK_HEADS

    @pl.when(pl.program_id(1) == 0)
    def _():
        st_ref[...] = jnp.zeros_like(st_ref)

    ti = lax.broadcasted_iota(jnp.int32, (C, LANES), 0)
    lane = lax.broadcasted_iota(jnp.int32, (C, LANES), 1)
    tj = lane & (C - 1)
    first = lane < C
    incl = tj <= ti
    strict = tj < ti
    eye = (ti == tj).astype(F32)
    n_chunks = q_ref.shape[0] // C
    chunks = range(n_chunks)
    rs = [slice(c * C, (c + 1) * C) for c in chunks]
    pairs = range(GD_QK_HEADS)
    qs = [slice(p * GD_DK, (p + 1) * GD_DK) for p in pairs]
    items = [(c, p) for c in chunks for p in pairs]
    W = GD_DV
    gb = [gb_ref[rs[c], :] for c in chunks]
    d_all = [_dot_exact01(tri_ref[...], gb[c]) for c in chunks]

    def col(x, h):
        return x[:, h:h + 1]

    def both(x, p, off=0):
        return jnp.where(first, col(x, off + 2 * p), col(x, off + 2 * p + 1))

    kk, qk, dec, t_inv, pw = {}, {}, {}, {}, {}
    for c, p in items:
        k2 = jnp.concatenate([k_ref[rs[c], qs[p]], k_ref[rs[c], qs[p]]], axis=0)
        kq = _dot_nt(jnp.concatenate([k_ref[rs[c], qs[p]], q_ref[rs[c], qs[p]]], axis=0), k2)
        kk[c, p] = kq[:C]
        qk[c, p] = kq[C:]
    for c, p in items:
        d_row = jnp.sum(jnp.where(ti <= tj, both(gb[c], p), 0.0), axis=0, keepdims=True)
        dec[c, p] = jnp.exp(jnp.where(incl, both(d_all[c], p) - d_row, -jnp.inf))
        a = kk[c, p] * both(gb[c], p, GD_V_HEADS) * jnp.where(strict, dec[c, p], 0.0)
        t_inv[c, p] = eye - a
        pw[c, p] = a
    rhs = {it: _pair_rhs(pw[it], first) for it in items}
    pw = {it: _pair_mm(_pair_lhs(pw[it]), rhs[it]) for it in items}
    for _ in range(int(math.log2(C)) - 2):
        rhs = {it: _pair_rhs(pw[it], first) for it in items}
        both = {it: _pair_mm(jnp.concatenate([_pair_lhs(t_inv[it]), _pair_lhs(pw[it])], axis=0), rhs[it])
                for it in items}
        t_inv = {it: t_inv[it] + both[it][:C] for it in items}
        pw = {it: both[it][C:] for it in items}
    rhs = {it: _pair_rhs(pw[it], first) for it in items}
    t_inv = {it: t_inv[it] + _pair_mm(_pair_lhs(t_inv[it]), rhs[it]) for it in items}
    uw, ed, kf = {}, {}, {}
    for c, p in items:
        kf[c, p] = k_ref[rs[c], qs[p]].astype(F32)
        halves = []
        for h in (2 * p, 2 * p + 1):
            ed[c, h] = jnp.exp(col(d_all[c], h))
            beta = col(gb[c], GD_V_HEADS + h)
            v = v_ref[rs[c], h * W:(h + 1) * W].astype(F32)
            halves.append(jnp.concatenate([v * beta, kf[c, p] * beta * ed[c, h]], axis=1).astype(BF16))
        uw[c, p] = _dot(t_inv[c, p].astype(BF16), _blockdiag(halves[0], halves[1]))
    for c in chunks:
        s2 = [_blockdiag(st_ref[2 * p].astype(BF16), st_ref[2 * p + 1].astype(BF16)) for p in pairs]
        vn, q_s = [], []
        for p in pairs:
            u2 = jnp.concatenate([uw[c, p][:, 0:W], uw[c, p][:, 2 * W:3 * W]], axis=1)
            w2 = jnp.concatenate([uw[c, p][:, W:2 * W], uw[c, p][:, 3 * W:4 * W]], axis=1).astype(BF16)
            q = q_ref[rs[c], qs[p]].astype(F32)
            qd = jnp.concatenate([q * ed[c, 2 * p], q * ed[c, 2 * p + 1]], axis=1).astype(BF16)
            ws_qs = _dot(jnp.concatenate([w2, qd], axis=0), s2[p])
            vn.append((u2 - ws_qs[:C]).astype(BF16))
            q_s.append(ws_qs[C:])
        for p in pairs:
            o = q_s[p] + _dot((qk[c, p] * dec[c, p]).astype(BF16), _blockdiag(vn[p][:, :W], vn[p][:, W:]))
            for i in range(2):
                oh = o[:, i * W:(i + 1) * W]
                on = oh * lax.rsqrt(jnp.mean(oh * oh, axis=-1, keepdims=True) + EPS) * og_ref[...]
                o_ref[rs[c], (2 * p + i) * W:(2 * p + i + 1) * W] = on.astype(o_ref.dtype)
        for p in pairs:
            for i in range(2):
                h = 2 * p + i
                d_last = col(d_all[c], h)[C - 1:C, :]
                kd = (kf[c, p] * jnp.exp(d_last - col(d_all[c], h))).astype(BF16)
                st_ref[h] = st_ref[h] * jnp.exp(d_last) + _dot_tn(kd, vn[p][:, i * W:(i + 1) * W])


def _gdn_layer(x2, mod, layer, norm_g, w_in, w_out, conv_w, a_log, dt_bias, onorm_g, B, T):
    tm = GD_ROW_TILE
    tt = T // tm
    N = B * T
    w_main = w_in[:, :GD_QKV + GD_VW].astype(BF16)
    w_ab = jnp.zeros((D_MODEL, LANES), F32).at[:, :2 * GD_V_HEADS].set(w_in[:, GD_QKV + GD_VW:])
    wab_hi = w_ab.astype(BF16)
    wab_lo = (w_ab - wab_hi.astype(F32)).astype(BF16)
    pad = lambda a: jnp.zeros((1, LANES), F32).at[0, :GD_V_HEADS].set(a.astype(F32))
    q, k, v, z, gb = pl.pallas_call(
        _gd_in_kernel,
        grid=(B, tt),
        in_specs=[_row_spec(tm, D_MODEL, tt), _const_spec((1, D_MODEL)), _mod_spec(layer),
                  _const_spec((D_MODEL, GD_QKV + GD_VW)), _const_spec((D_MODEL, LANES)),
                  _const_spec((D_MODEL, LANES)), _const_spec((GD_CONV, GD_QKV)),
                  _const_spec((1, LANES)), _const_spec((1, LANES))],
        out_specs=[_row_spec(tm, GD_QKW, tt), _row_spec(tm, GD_QKW, tt), _row_spec(tm, GD_VW, tt),
                   _row_spec(tm, GD_VW, tt), _row_spec(tm, LANES, tt)],
        out_shape=[jax.ShapeDtypeStruct((N, GD_QKW), BF16), jax.ShapeDtypeStruct((N, GD_QKW), BF16),
                   jax.ShapeDtypeStruct((N, GD_VW), BF16), jax.ShapeDtypeStruct((N, GD_VW), BF16),
                   jax.ShapeDtypeStruct((N, LANES), F32)],
        scratch_shapes=[pltpu.VMEM((SUBLANES, GD_QKV), F32), pltpu.VMEM((tm + SUBLANES, D_MODEL), F32)],
        compiler_params=_params(2),
        name=f"gdn_in_{layer}",
    )(x2, norm_g.reshape(1, D_MODEL), mod, w_main, wab_hi, wab_lo, conv_w.astype(F32),
      pad(a_log), pad(dt_bias))

    C = GD_CHUNK
    rows = GD_STEP_CHUNKS * C
    tc = T // rows
    tri = jnp.asarray(np.tril(np.ones((C, C), np.float32)), BF16)
    o = pl.pallas_call(
        _gd_scan_kernel,
        grid=(B, tc),
        in_specs=[_row_spec(rows, GD_QKW, tc), _row_spec(rows, GD_QKW, tc), _row_spec(rows, GD_VW, tc),
                  _row_spec(rows, LANES, tc), _const_spec((C, C)), _const_spec((1, GD_DV))],
        out_specs=_row_spec(rows, GD_VW, tc),
        out_shape=jax.ShapeDtypeStruct((N, GD_VW), BF16),
        scratch_shapes=[pltpu.VMEM((GD_V_HEADS, GD_DK, GD_DV), F32)],
        compiler_params=_params(2),
        name=f"gdn_scan_{layer}",
    )(q, k, v, gb, tri, onorm_g.reshape(1, GD_DV))
    return _out_proj(o, z, x2, mod, layer, w_out, B, T)


def kernel(x, c, positions, hgrn_lb, ada_w, ada_b, norm_g, hg_in_w, hg_out_w, hg_onorm, sw_in_w, sw_out_w,
           sw_qnorm, sw_knorm, sw_sinks, gd_in_w, gd_out_w, gd_conv_w, gd_a_log, gd_dt_bias, gd_onorm):
    B, T, _ = x.shape
    lb_all = jnp.cumsum(jax.nn.softmax(hgrn_lb.astype(F32), axis=0), axis=0)
    lb_all = lb_all - lb_all[0:1]
    mod = _ada_mod(c, ada_w, ada_b)
    x2 = x.reshape(B * T, D_MODEL)
    for i in range(DEPTH):
        j, kind = divmod(i, 3)
        if kind == 0:
            x2 = _hgrn2_layer(x2, mod, i, norm_g[i], lb_all[i], hg_in_w[j], hg_out_w[j], hg_onorm[j], B, T)
        elif kind == 1:
            x2 = _swa_layer(x2, mod, i, norm_g[i], positions, sw_in_w[j], sw_out_w[j], sw_qnorm[j],
                            sw_knorm[j], sw_sinks[j], B, T)
        else:
            x2 = _gdn_layer(x2, mod, i, norm_g[i], gd_in_w[j], gd_out_w[j], gd_conv_w[j], gd_a_log[j],
                            gd_dt_bias[j], gd_onorm[j], B, T)
    return x2.reshape(B, T, D_MODEL)
```

```python
import functools
import math

import numpy as np
import jax
import jax.numpy as jnp
from jax import lax
from jax.experimental import pallas as pl
from jax.experimental.pallas import tpu as pltpu

F32 = jnp.float32
BF16 = jnp.bfloat16

D_MODEL = 1024
DEPTH = 4
EPS = 1e-6
LANES = 128
SUBLANES = 8
VMEM_LIMIT = 56 * 2**20

HG_HEADS = 8
HG_DK = 128
HG_CHUNK = 64
HG_STEP_CHUNKS = 4

SW_HEADS = 16
SW_KV_HEADS = 4
SW_DH = 64
SW_BLOCK = 128
SW_QW = SW_HEADS * SW_DH
SW_KVW = SW_KV_HEADS * SW_DH
ROPE_THETA = 10000.0

GD_QK_HEADS = 8
GD_V_HEADS = 16
GD_DK = 128
GD_DV = 128
GD_CONV = 4
GD_CHUNK = 64
GD_STEP_CHUNKS = 2
GD_QKW = GD_QK_HEADS * GD_DK
GD_VW = GD_V_HEADS * GD_DV
GD_QKV = 2 * GD_QKW + GD_VW

ROW_TILE = 512
GD_ROW_TILE = 512
GD_CONV_ROWS = 64


def _params(n_axes):
    return pltpu.CompilerParams(dimension_semantics=("arbitrary",) * n_axes,
                                vmem_limit_bytes=VMEM_LIMIT)


def _dot(a, b):
    return jnp.dot(a, b, preferred_element_type=F32)


def _dot_nt(a, b):
    return lax.dot_general(a, b, (((1,), (1,)), ((), ())), preferred_element_type=F32)


def _dot_tn(a, b):
    return lax.dot_general(a, b, (((0,), (0,)), ((), ())), preferred_element_type=F32)


def _split2(x):
    hi = x.astype(BF16)
    lo = (x - hi.astype(F32)).astype(BF16)
    return hi, lo


def _dot_exact01(m01, x):
    hi, lo = _split2(x)
    return _dot(m01, hi) + _dot(m01, lo)


def _sigmoid(x):
    return 0.5 * jnp.tanh(0.5 * x) + 0.5


def _silu(x):
    h = 0.5 * x
    return h * jnp.tanh(h) + h


def _ada_kernel(c_ref, w_ref, b_ref, o_ref):
    o_ref[...] = jnp.dot(c_ref[...], w_ref[...], precision=lax.Precision.HIGHEST,
                         preferred_element_type=F32) + b_ref[...]


def _ada_mod(c, ada_w, ada_b):
    B = c.shape[0]
    depth = ada_w.shape[0]
    c8 = jnp.zeros((SUBLANES, D_MODEL), F32).at[:B].set(c)
    return pl.pallas_call(
        _ada_kernel,
        grid=(depth, 3),
        in_specs=[pl.BlockSpec((SUBLANES, D_MODEL), lambda i, j: (0, 0)),
                  pl.BlockSpec((None, D_MODEL, D_MODEL), lambda i, j: (i, 0, j)),
                  pl.BlockSpec((None, 1, D_MODEL), lambda i, j: (i, 0, j))],
        out_specs=pl.BlockSpec((None, SUBLANES, D_MODEL), lambda i, j: (i, 0, j)),
        out_shape=jax.ShapeDtypeStruct((depth, SUBLANES, 3 * D_MODEL), F32),
        compiler_params=_params(2),
        name="ada_mod",
    )(c8, ada_w, ada_b.reshape(depth, 1, 3 * D_MODEL))


def _prenorm(x_ref, g_ref, mod_ref):
    b = pl.program_id(0)
    x = x_ref[...]
    y = x * lax.rsqrt(jnp.mean(x * x, axis=-1, keepdims=True) + EPS) * g_ref[...]
    m = mod_ref[pl.ds(b, 1), :]
    shift = m[:, :D_MODEL]
    scale = m[:, D_MODEL:2 * D_MODEL]
    return (y * (1.0 + scale) + shift).astype(BF16)


def _row_spec(tm, width, t_tiles):
    return pl.BlockSpec((tm, width), lambda b, i: (b * t_tiles + i, 0))


def _const_spec(shape):
    return pl.BlockSpec(shape, lambda b, i: (0,) * len(shape))


def _mod_spec(layer):
    return pl.BlockSpec((None, SUBLANES, 3 * D_MODEL), lambda b, i: (layer, 0, 0))


def _out_kernel(o_ref, z_ref, x_ref, mod_ref, w_ref, out_ref):
    b = pl.program_id(0)
    z = z_ref[...].astype(F32)
    y = (o_ref[...].astype(F32) * _silu(z)).astype(BF16)
    acc = _dot(y, w_ref[...])
    gate = mod_ref[pl.ds(b, 1), :][:, 2 * D_MODEL:]
    out_ref[...] = x_ref[...] + gate * acc


def _out_proj(o, z, x2, mod, layer, w_out, B, T):
    tm = ROW_TILE
    tt = T // tm
    wo = w_out.shape[0]
    return pl.pallas_call(
        _out_kernel,
        grid=(B, tt),
        in_specs=[_row_spec(tm, wo, tt), _row_spec(tm, wo, tt), _row_spec(tm, D_MODEL, tt),
                  _mod_spec(layer), _const_spec((wo, D_MODEL))],
        out_specs=_row_spec(tm, D_MODEL, tt),
        out_shape=jax.ShapeDtypeStruct((B * T, D_MODEL), F32),
        compiler_params=_params(2),
        name=f"out_proj_{layer}",
    )(o, z, x2, mod, w_out.astype(BF16))


def _hg_in_kernel(x_ref, g_ref, mod_ref, w_ref, lb_ref, q_ref, k_ref, v_ref, z_ref, lf_ref):
    h = _prenorm(x_ref, g_ref, mod_ref)
    lb = lb_ref[...]
    qp = _dot(h, w_ref[:, 0:D_MODEL])
    q_ref[...] = _silu(qp).astype(BF16)
    fp = _dot(h, w_ref[:, D_MODEL:2 * D_MODEL])
    th = 0.5 * jnp.tanh(0.5 * fp)
    lf_ref[...] = jnp.log(lb + (1.0 - lb) * (0.5 + th))
    k_ref[...] = ((1.0 - lb) * (0.5 - th)).astype(BF16)
    v_ref[...] = _dot(h, w_ref[:, 2 * D_MODEL:3 * D_MODEL]).astype(BF16)
    z_ref[...] = _dot(h, w_ref[:, 3 * D_MODEL:4 * D_MODEL]).astype(BF16)


def _hg_levels(L):
    m, out = L // 2, []
    while m >= 1:
        out.append(m)
        m //= 2
    return out


def _hg_mxu_levels(L):
    return [m for m in _hg_levels(L) if 1 < m < SUBLANES]


def _hg_tables(L):
    t = np.arange(L)[:, None]
    u = np.arange(L)[None, :]
    masks = [np.eye(L, dtype=bool)]
    for m in _hg_levels(L):
        r = (t // (2 * m)) * (2 * m) + m
        ru = (u // (2 * m)) * (2 * m) + m
        masks.append((t // (2 * m) == u // (2 * m)) & (t >= r) & (u < ru))
    secs = [u <= t]
    for m in _hg_mxu_levels(L):
        r = (t // (2 * m)) * (2 * m) + m
        secs.append(np.where(t >= r, (u >= r) & (u <= t), (u > t) & (u < r)))
    cum = np.concatenate(secs, axis=0).astype(np.float32)
    return jnp.asarray(cum, BF16), jnp.asarray(np.stack(masks).astype(np.float32))


def _hg_exponents(lf, sums, ex_ref, b_ref, L, levels):
    G = SUBLANES
    width = lf.shape[1]
    mxu_levels = _hg_mxu_levels(L)
    b_ref[...] = sums[0:L]
    odd = (lax.broadcasted_iota(jnp.int32, (L, width), 0) & 1) != 0
    for li, m in enumerate(levels):
        dst = slice((2 + li) * L, (3 + li) * L)
        if m in mxu_levels:
            k = 1 + mxu_levels.index(m)
            ex_ref[dst, :] = jnp.exp(sums[k * L:(k + 1) * L])
        elif m == 1:
            ex_ref[dst, :] = jnp.exp(jnp.where(odd, lf, 0.0))
    rows = {}

    def brow(r):
        if r not in rows:
            rows[r] = jnp.broadcast_to(b_ref[r:r + 1, :], (G, width))
        return rows[r]

    for g in range(L // G):
        base = g * G
        b = b_ref[base:base + G, :]
        ex_ref[base:base + G, :] = jnp.exp(b)
        ex_ref[L + base:L + base + G, :] = jnp.exp(brow(L - 1) - b)
        for li, m in enumerate(levels):
            if m >= G:
                ref_row = (base // (2 * m)) * (2 * m) + m - 1
                d = b - brow(ref_row)
                ex_ref[(2 + li) * L + base:(2 + li) * L + base + G, :] = jnp.exp(d if base % (2 * m) >= m else -d)


def _hg_scan_kernel(q_ref, k_ref, v_ref, lf_ref, cum_ref, msk_ref, og_ref, o_ref, st_ref, ex_ref, b_ref, *, L):
    levels = _hg_levels(L)
    n_chunks = q_ref.shape[0] // L

    @pl.when(pl.program_id(1) == 0)
    def _():
        st_ref[...] = jnp.zeros_like(st_ref)

    row = lax.broadcasted_iota(jnp.int32, (L, HG_DK), 0)
    og = og_ref[...]
    heads = range(HG_HEADS)
    chunks = range(n_chunks)
    hs = [slice(h * HG_DK, (h + 1) * HG_DK) for h in heads]
    rs = [slice(c * L, (c + 1) * L) for c in chunks]
    for c in chunks:
        sums = _dot_exact01(cum_ref[...], lf_ref[rs[c], :])
        _hg_exponents(lf_ref[rs[c], :], sums, ex_ref.at[c], b_ref.at[c], L, levels)
    a = [[msk_ref[0] * _dot_nt(q_ref[rs[c], hs[h]], k_ref[rs[c], hs[h]]) for h in heads] for c in chunks]
    for li, m in enumerate(levels):
        upper = (row & m) != 0
        for c in chunks:
            for h in heads:
                e = ex_ref[c, (2 + li) * L:(3 + li) * L, hs[h]]
                x = jnp.where(upper, q_ref[rs[c], hs[h]].astype(F32), k_ref[rs[c], hs[h]].astype(F32))
                xm = (x * e).astype(BF16)
                a[c][h] = a[c][h] + msk_ref[li + 1] * _dot_nt(xm, xm)
    av = [[_dot(a[c][h].astype(BF16), v_ref[rs[c], hs[h]]) for h in heads] for c in chunks]
    for c in chunks:
        for h in heads:
            qd = (q_ref[rs[c], hs[h]].astype(F32) * ex_ref[c, 0:L, hs[h]]).astype(BF16)
            o = av[c][h] + _dot_nt(qd, st_ref[h].astype(BF16))
            on = o * lax.rsqrt(jnp.mean(o * o, axis=-1, keepdims=True) + EPS) * og
            o_ref[rs[c], hs[h]] = on.astype(o_ref.dtype)
        for h in heads:
            kd = (k_ref[rs[c], hs[h]].astype(F32) * ex_ref[c, L:2 * L, hs[h]]).astype(BF16)
            st_ref[h] = st_ref[h] * ex_ref[c, L - 1:L, hs[h]] + _dot_tn(v_ref[rs[c], hs[h]], kd)


def _hgrn2_layer(x2, mod, layer, norm_g, lb, w_in, w_out, onorm_g, B, T):
    tm = ROW_TILE
    tt = T // tm
    N = B * T
    q, k, v, z, lf = pl.pallas_call(
        _hg_in_kernel,
        grid=(B, tt),
        in_specs=[_row_spec(tm, D_MODEL, tt), _const_spec((1, D_MODEL)), _mod_spec(layer),
                  _const_spec((D_MODEL, 4 * D_MODEL)), _const_spec((1, D_MODEL))],
        out_specs=[_row_spec(tm, D_MODEL, tt)] * 5,
        out_shape=[jax.ShapeDtypeStruct((N, D_MODEL), BF16)] * 4 + [jax.ShapeDtypeStruct((N, D_MODEL), F32)],
        compiler_params=_params(2),
        name=f"hgrn2_in_{layer}",
    )(x2, norm_g.reshape(1, D_MODEL), mod, w_in.astype(BF16), lb.reshape(1, D_MODEL))

    L = HG_CHUNK
    nc = HG_STEP_CHUNKS
    tc = T // (nc * L)
    cum, msk = _hg_tables(L)
    o = pl.pallas_call(
        functools.partial(_hg_scan_kernel, L=L),
        grid=(B, tc),
        in_specs=[_row_spec(nc * L, D_MODEL, tc)] * 4 + [_const_spec(cum.shape), _const_spec(msk.shape),
                                                          _const_spec((1, HG_DK))],
        out_specs=_row_spec(nc * L, D_MODEL, tc),
        out_shape=jax.ShapeDtypeStruct((N, D_MODEL), BF16),
        scratch_shapes=[pltpu.VMEM((HG_HEADS, HG_DK, HG_DK), F32),
                        pltpu.VMEM((nc, (2 + len(_hg_levels(L))) * L, D_MODEL), F32),
                        pltpu.VMEM((nc, L, D_MODEL), F32)],
        compiler_params=_params(2),
        name=f"hgrn2_scan_{layer}",
    )(q, k, v, lf, cum, msk, onorm_g.reshape(1, HG_DK))
    return _out_proj(o, z, x2, mod, layer, w_out, B, T)


def _sw_in_kernel(x_ref, g_ref, mod_ref, w_ref, pos_ref, invf_ref, qg_ref, kg_ref, seg_ref,
                  q_ref, kd_ref, vd_ref, z_ref):
    h = _prenorm(x_ref, g_ref, mod_ref)
    tm = h.shape[0]
    ang = invf_ref[...] * pos_ref[...].astype(F32)
    cos_f = jnp.cos(ang)
    sin_f = jnp.sin(ang)
    cs = jnp.concatenate([cos_f, sin_f, cos_f, sin_f], axis=0).T
    lane = lax.broadcasted_iota(jnp.int32, (tm, LANES), 1)
    first = (lane & (SW_DH // 2)) == 0
    cos_t = jnp.where(first, cs, pltpu.roll(cs, SW_DH // 2, 1))
    sin_a = jnp.where(first, -pltpu.roll(cs, LANES - SW_DH // 2, 1), 0.0)
    sin_b = jnp.where(first, 0.0, cs)
    low = lane < SW_DH
    seg = seg_ref[...]
    W2 = 2 * LANES

    xq = _dot(h, w_ref[:, 0:SW_QW])
    xk = _dot(h, w_ref[:, SW_QW:SW_QW + SW_KVW])
    xv = _dot(h, w_ref[:, SW_QW + SW_KVW:SW_QW + 2 * SW_KVW])
    z_ref[...] = _dot(h, w_ref[:, SW_QW + 2 * SW_KVW:]).astype(BF16)
    ms_q = [_dot((xq[:, c * W2:(c + 1) * W2] * xq[:, c * W2:(c + 1) * W2]).astype(BF16), seg)
            for c in range(SW_QW // W2)]
    ms_k = [_dot((xk[:, c * W2:(c + 1) * W2] * xk[:, c * W2:(c + 1) * W2]).astype(BF16), seg)
            for c in range(SW_KVW // W2)]

    def norm_rope(xg, ms, gain):
        xn = xg * lax.rsqrt(ms + EPS) * gain
        return (xn * cos_t + pltpu.roll(xn, LANES - SW_DH // 2, 1) * sin_a
                + pltpu.roll(xn, SW_DH // 2, 1) * sin_b)

    def group(x, ms, c):
        half = (c % 2) * LANES
        return x[:, c * LANES:(c + 1) * LANES], ms[c // 2][:, half:half + LANES]

    scale = SW_DH ** -0.5
    for c in range(SW_QW // LANES):
        xg, ms = group(xq, ms_q, c)
        q_ref[:, c * LANES:(c + 1) * LANES] = (norm_rope(xg, ms, qg_ref[...]) * scale).astype(BF16)
    for c in range(SW_KVW // LANES):
        xg, ms = group(xk, ms_k, c)
        kr = norm_rope(xg, ms, kg_ref[...])
        sw = pltpu.roll(kr, SW_DH, 1)
        for hh, (in_low, in_high) in enumerate(((kr, sw), (sw, kr))):
            base = (2 * c + hh) * 2 * LANES
            kd_ref[:, base:base + LANES] = jnp.where(low, in_low, 0.0).astype(BF16)
            kd_ref[:, base + LANES:base + 2 * LANES] = jnp.where(low, 0.0, in_high).astype(BF16)
        vg = xv[:, c * LANES:(c + 1) * LANES]
        sv = pltpu.roll(vg, SW_DH, 1)
        vd_ref[:, (2 * c) * LANES:(2 * c + 1) * LANES] = jnp.where(low, vg, sv).astype(BF16)
        vd_ref[:, (2 * c + 1) * LANES:(2 * c + 2) * LANES] = jnp.where(low, sv, vg).astype(BF16)


def _sw_attn_kernel(sink_ref, q_ref, kc_ref, kp_ref, vc_ref, vp_ref, o_ref):
    n = pl.program_id(1)
    BLK = SW_BLOCK
    qi = lax.broadcasted_iota(jnp.int32, (BLK, 2 * BLK), 0)
    kj = lax.broadcasted_iota(jnp.int32, (BLK, 2 * BLK), 1)
    allowed = ((kj < BLK) & (kj > qi) & (n > 0)) | ((kj >= BLK) & ((kj - BLK) <= qi))
    bias = jnp.where(allowed, 0.0, -jnp.inf)
    lane = lax.broadcasted_iota(jnp.int32, (BLK, LANES), 1)
    low = lane < SW_DH
    kv_heads = range(SW_KV_HEADS)
    n_pairs = SW_HEADS // SW_KV_HEADS // 2
    bias = jnp.concatenate([bias] * n_pairs, axis=0)
    row_pair = lax.broadcasted_iota(jnp.int32, (n_pairs * BLK, 1), 0) // BLK
    scores = {}
    for h in kv_heads:
        q2 = q_ref[:, h * n_pairs * LANES:(h + 1) * n_pairs * LANES]
        q2 = jnp.concatenate([q2[:, j * LANES:(j + 1) * LANES] for j in range(n_pairs)], axis=0)
        for half in range(2):
            ks = slice((2 * h + half) * LANES, (2 * h + half + 1) * LANES)
            kk = jnp.concatenate([kp_ref[:, ks], kc_ref[:, ks]], axis=0)
            scores[h, half] = _dot_nt(q2, kk) + bias
    probs, denom = {}, {}
    for h in kv_heads:
        for half in range(2):
            sink = sink_ref[2 * h * n_pairs + half]
            for j in range(1, n_pairs):
                sink = jnp.where(row_pair == j, sink_ref[2 * (h * n_pairs + j) + half], sink)
            s = scores[h, half]
            m = jnp.maximum(jnp.max(s, axis=-1, keepdims=True), sink)
            p = jnp.exp(s - m)
            denom[h, half] = jnp.sum(p, axis=-1, keepdims=True) + jnp.exp(sink - m)
            probs[h, half] = p.astype(BF16)
    pv = {}
    for h in kv_heads:
        hs = slice(h * LANES, (h + 1) * LANES)
        vv = jnp.concatenate([vp_ref[:, hs], vc_ref[:, hs]], axis=0)
        pv[h] = _dot(jnp.concatenate([probs[h, 0], probs[h, 1]], axis=0), vv)
    R = n_pairs * BLK
    for h in kv_heads:
        o0 = pv[h][:R] / denom[h, 0]
        o1 = pv[h][R:] / denom[h, 1]
        for j in range(n_pairs):
            c = h * n_pairs + j
            o_ref[:, c * LANES:(c + 1) * LANES] = jnp.where(
                low, o0[j * BLK:(j + 1) * BLK], o1[j * BLK:(j + 1) * BLK]).astype(o_ref.dtype)


def _swa_layer(x2, mod, layer, norm_g, positions, w_in, w_out, qn_g, kn_g, sinks, B, T):
    tm = ROW_TILE
    tt = T // tm
    N = B * T
    inv_freq = ROPE_THETA ** (-jnp.arange(0, SW_DH, 2, dtype=F32) / SW_DH)
    invf = inv_freq.reshape(SW_DH // 2, 1)
    seg = jnp.asarray(np.kron(np.eye(2 * LANES // SW_DH), np.full((SW_DH, SW_DH), 1.0 / SW_DH)), BF16)
    tile2 = lambda g: jnp.tile(g.astype(F32), LANES // SW_DH).reshape(1, LANES)
    q, kd, vd, z = pl.pallas_call(
        _sw_in_kernel,
        grid=(B, tt),
        in_specs=[_row_spec(tm, D_MODEL, tt), _const_spec((1, D_MODEL)), _mod_spec(layer),
                  _const_spec((D_MODEL, w_in.shape[1])),
                  pl.BlockSpec((None, 1, tm), lambda b, i: (b * tt + i, 0, 0)),
                  _const_spec((SW_DH // 2, 1)), _const_spec((1, LANES)), _const_spec((1, LANES)),
                  _const_spec((2 * LANES, 2 * LANES))],
        out_specs=[_row_spec(tm, SW_QW, tt), _row_spec(tm, 4 * SW_KVW, tt),
                   _row_spec(tm, 2 * SW_KVW, tt), _row_spec(tm, SW_QW, tt)],
        out_shape=[jax.ShapeDtypeStruct((N, SW_QW), BF16), jax.ShapeDtypeStruct((N, 4 * SW_KVW), BF16),
                   jax.ShapeDtypeStruct((N, 2 * SW_KVW), BF16), jax.ShapeDtypeStruct((N, SW_QW), BF16)],
        compiler_params=_params(2),
        name=f"swa_in_{layer}",
    )(x2, norm_g.reshape(1, D_MODEL), mod, w_in.astype(BF16), positions.reshape(N // tm, 1, tm),
      invf, tile2(qn_g), tile2(kn_g), seg)

    nb = T // SW_BLOCK
    cur = lambda b, n: (b * nb + n, 0)
    prev = lambda b, n: (b * nb + jnp.maximum(n - 1, 0), 0)
    kw, vw = 4 * SW_KVW, 2 * SW_KVW
    o = pl.pallas_call(
        _sw_attn_kernel,
        grid=(B, nb),
        in_specs=[pl.BlockSpec(memory_space=pltpu.SMEM),
                  pl.BlockSpec((SW_BLOCK, SW_QW), cur),
                  pl.BlockSpec((SW_BLOCK, kw), cur), pl.BlockSpec((SW_BLOCK, kw), prev),
                  pl.BlockSpec((SW_BLOCK, vw), cur), pl.BlockSpec((SW_BLOCK, vw), prev)],
        out_specs=pl.BlockSpec((SW_BLOCK, SW_QW), cur),
        out_shape=jax.ShapeDtypeStruct((N, SW_QW), BF16),
        compiler_params=_params(2),
        name=f"swa_attn_{layer}",
    )(sinks.astype(F32), q, kd, kd, vd, vd)
    return _out_proj(o, z, x2, mod, layer, w_out, B, T)


def _gd_in_kernel(x_ref, g_ref, mod_ref, w_ref, wab_hi_ref, wab_lo_ref, cw_ref, alog_ref, dtb_ref,
                  q_ref, k_ref, v_ref, z_ref, gb_ref, hist_ref, buf_ref):
    tm = x_ref.shape[0]
    H = SUBLANES

    @pl.when(pl.program_id(1) == 0)
    def _():
        hist_ref[...] = jnp.zeros_like(hist_ref)

    x = x_ref[...]
    b = pl.program_id(0)
    y = x * lax.rsqrt(jnp.mean(x * x, axis=-1, keepdims=True) + EPS) * g_ref[...]
    m = mod_ref[pl.ds(b, 1), :]
    hf = y * (1.0 + m[:, D_MODEL:2 * D_MODEL]) + m[:, :D_MODEL]
    h = hf.astype(BF16)
    h_lo = (hf - h.astype(F32)).astype(BF16)

    R = GD_CONV_ROWS
    sub = lax.broadcasted_iota(jnp.int32, (R // H, H, D_MODEL), 1)
    for s in range(GD_QKV // D_MODEL):
        cs = slice(s * D_MODEL, (s + 1) * D_MODEL)
        buf_ref[0:H, :] = hist_ref[:, cs]
        buf_ref[H:H + tm, :] = _dot(h, w_ref[:, cs])
        taps = [cw_ref[GD_CONV - 1 - j:GD_CONV - j, cs] for j in range(GD_CONV)]
        for r0 in range(0, tm, R):
            rows = slice(r0, r0 + R)
            grp = buf_ref[r0:r0 + R + H, :].reshape(R // H + 1, H, D_MODEL)
            conv = taps[0] * buf_ref[H + r0:H + r0 + R, :]
            for j in range(1, GD_CONV):
                rot = pltpu.roll(grp, j, 1)
                conv = conv + taps[j] * jnp.where(sub < j, rot[:-1], rot[1:]).reshape(R, D_MODEL)
            act = _silu(conv)
            if s < 2:
                for c in range(D_MODEL // GD_DK):
                    a = act[:, c * GD_DK:(c + 1) * GD_DK]
                    r = lax.rsqrt(jnp.sum(a * a, axis=-1, keepdims=True) + EPS)
                    if s == 0:
                        q_ref[rows, c * GD_DK:(c + 1) * GD_DK] = (a * (r * GD_DK ** -0.5)).astype(BF16)
                    else:
                        k_ref[rows, c * GD_DK:(c + 1) * GD_DK] = (a * r).astype(BF16)
            else:
                v_ref[rows, (s - 2) * D_MODEL:(s - 1) * D_MODEL] = act.astype(BF16)
        hist_ref[:, cs] = buf_ref[tm:tm + H, :]
    z_ref[...] = _dot(h, w_ref[:, GD_QKV:GD_QKV + GD_VW]).astype(BF16)

    ab = _dot(h, wab_hi_ref[...]) + _dot(h, wab_lo_ref[...]) + _dot(h_lo, wab_hi_ref[...])
    xa = ab + dtb_ref[...]
    softplus = jnp.maximum(xa, 0.0) + jnp.log(1.0 + jnp.exp(-jnp.abs(xa)))
    gv = -jnp.exp(alog_ref[...]) * softplus
    lane = lax.broadcasted_iota(jnp.int32, (tm, LANES), 1)
    gb_ref[...] = jnp.where(lane < GD_V_HEADS, gv, _sigmoid(ab))


def _blockdiag(a, b):
    return jnp.concatenate([jnp.concatenate([a, jnp.zeros_like(b)], axis=1),
                            jnp.concatenate([jnp.zeros_like(a), b], axis=1)], axis=0)


def _pair_lhs(xp):
    hi, lo = _split2(xp)
    return jnp.concatenate([hi, lo], axis=1)


def _pair_rhs(yp, first):
    hi32 = yp.astype(BF16).astype(F32)
    lo32 = yp - hi32
    a_hi = jnp.where(first, hi32, 0.0).astype(BF16)
    b_hi = jnp.where(first, 0.0, hi32).astype(BF16)
    a_lo = jnp.where(first, lo32, 0.0).astype(BF16)
    b_lo = jnp.where(first, 0.0, lo32).astype(BF16)
    z = jnp.zeros_like(a_hi)
    return jnp.concatenate([jnp.concatenate([a_hi, b_hi, a_hi, b_hi], axis=0),
                            jnp.concatenate([a_lo, b_lo, z, z], axis=0)], axis=1)


def _pair_mm(lhs, rhs):
    o = _dot(lhs, rhs)
    return o[:, :LANES] + o[:, LANES:]


def _gd_scan_kernel(q_ref, k_ref, v_ref, gb_ref, tri_ref, og_ref, o_ref, st_ref):
    C = GD_CHUNK
    assert 2 * C == LANES and GD_V_HEADS == 2 * GD_QK_HEADS

    @pl.when(pl.program_id(1) == 0)
    def _():
        st_ref[...] = jnp.zeros_like(st_ref)

    ti = lax.broadcasted_iota(jnp.int32, (C, LANES), 0)
    lane = lax.broadcasted_iota(jnp.int32, (C, LANES), 1)
    tj = lane & (C - 1)
    first = lane < C
    incl = tj <= ti
    strict = tj < ti
    eye = (ti == tj).astype(F32)
    n_chunks = q_ref.shape[0] // C
    chunks = range(n_chunks)
    rs = [slice(c * C, (c + 1) * C) for c in chunks]
    pairs = range(GD_QK_HEADS)
    qs = [slice(p * GD_DK, (p + 1) * GD_DK) for p in pairs]
    items = [(c, p) for c in chunks for p in pairs]
    W = GD_DV
    gb = [gb_ref[rs[c], :] for c in chunks]
    d_all = [_dot_exact01(tri_ref[...], gb[c]) for c in chunks]

    def col(x, h):
        return x[:, h:h + 1]

    def both(x, p, off=0):
        return jnp.where(first, col(x, off + 2 * p), col(x, off + 2 * p + 1))

    kk, qk, dec, t_inv, pw = {}, {}, {}, {}, {}
    for c, p in items:
        k2 = jnp.concatenate([k_ref[rs[c], qs[p]], k_ref[rs[c], qs[p]]], axis=0)
        kq = _dot_nt(jnp.concatenate([k_ref[rs[c], qs[p]], q_ref[rs[c], qs[p]]], axis=0), k2)
        kk[c, p] = kq[:C]
        qk[c, p] = kq[C:]
    for c, p in items:
        d_row = jnp.sum(jnp.where(ti <= tj, both(gb[c], p), 0.0), axis=0, keepdims=True)
        dec[c, p] = jnp.exp(jnp.where(incl, both(d_all[c], p) - d_row, -jnp.inf))
        a = kk[c, p] * both(gb[c], p, GD_V_HEADS) * jnp.where(strict, dec[c, p], 0.0)
        t_inv[c, p] = eye - a
        pw[c, p] = a
    rhs = {it: _pair_rhs(pw[it], first) for it in items}
    pw = {it: _pair_mm(_pair_lhs(pw[it]), rhs[it]) for it in items}
    for _ in range(int(math.log2(C)) - 2):
        rhs = {it: _pair_rhs(pw[it], first) for it in items}
        both = {it: _pair_mm(jnp.concatenate([_pair_lhs(t_inv[it]), _pair_lhs(pw[it])], axis=0), rhs[it])
                for it in items}
        t_inv = {it: t_inv[it] + both[it][:C] for it in items}
        pw = {it: both[it][C:] for it in items}
    rhs = {it: _pair_rhs(pw[it], first) for it in items}
    t_inv = {it: t_inv[it] + _pair_mm(_pair_lhs(t_inv[it]), rhs[it]) for it in items}
    uw, ed, kf = {}, {}, {}
    for c, p in items:
        kf[c, p] = k_ref[rs[c], qs[p]].astype(F32)
        halves = []
        for h in (2 * p, 2 * p + 1):
            ed[c, h] = jnp.exp(col(d_all[c], h))
            beta = col(gb[c], GD_V_HEADS + h)
            v = v_ref[rs[c], h * W:(h + 1) * W].astype(F32)
            halves.append(jnp.concatenate([v * beta, kf[c, p] * beta * ed[c, h]], axis=1).astype(BF16))
        uw[c, p] = _dot(t_inv[c, p].astype(BF16), _blockdiag(halves[0], halves[1]))
    for c in chunks:
        s2 = [_blockdiag(st_ref[2 * p].astype(BF16), st_ref[2 * p + 1].astype(BF16)) for p in pairs]
        vn, q_s = [], []
        for p in pairs:
            u2 = jnp.concatenate([uw[c, p][:, 0:W], uw[c, p][:, 2 * W:3 * W]], axis=1)
            w2 = jnp.concatenate([uw[c, p][:, W:2 * W], uw[c, p][:, 3 * W:4 * W]], axis=1).astype(BF16)
            q = q_ref[rs[c], qs[p]].astype(F32)
            qd = jnp.concatenate([q * ed[c, 2 * p], q * ed[c, 2 * p + 1]], axis=1).astype(BF16)
            ws_qs = _dot(jnp.concatenate([w2, qd], axis=0), s2[p])
            vn.append((u2 - ws_qs[:C]).astype(BF16))
            q_s.append(ws_qs[C:])
        for p in pairs:
            o = q_s[p] + _dot((qk[c, p] * dec[c, p]).astype(BF16), _blockdiag(vn[p][:, :W], vn[p][:, W:]))
            for i in range(2):
                oh = o[:, i * W:(i + 1) * W]
                on = oh * lax.rsqrt(jnp.mean(oh * oh, axis=-1, keepdims=True) + EPS) * og_ref[...]
                o_ref[rs[c], (2 * p + i) * W:(2 * p + i + 1) * W] = on.astype(o_ref.dtype)
        for p in pairs:
            for i in range(2):
                h = 2 * p + i
                d_last = col(d_all[c], h)[C - 1:C, :]
                kd = (kf[c, p] * jnp.exp(d_last - col(d_all[c], h))).astype(BF16)
                st_ref[h] = st_ref[h] * jnp.exp(d_last) + _dot_tn(kd, vn[p][:, i * W:(i + 1) * W])


def _gdn_layer(x2, mod, layer, norm_g, w_in, w_out, conv_w, a_log, dt_bias, onorm_g, B, T):
    tm = GD_ROW_TILE
    tt = T // tm
    N = B * T
    w_main = w_in[:, :GD_QKV + GD_VW].astype(BF16)
    w_ab = jnp.zeros((D_MODEL, LANES), F32).at[:, :2 * GD_V_HEADS].set(w_in[:, GD_QKV + GD_VW:])
    wab_hi = w_ab.astype(BF16)
    wab_lo = (w_ab - wab_hi.astype(F32)).astype(BF16)
    pad = lambda a: jnp.zeros((1, LANES), F32).at[0, :GD_V_HEADS].set(a.astype(F32))
    q, k, v, z, gb = pl.pallas_call(
        _gd_in_kernel,
        grid=(B, tt),
        in_specs=[_row_spec(tm, D_MODEL, tt), _const_spec((1, D_MODEL)), _mod_spec(layer),
                  _const_spec((D_MODEL, GD_QKV + GD_VW)), _const_spec((D_MODEL, LANES)),
                  _const_spec((D_MODEL, LANES)), _const_spec((GD_CONV, GD_QKV)),
                  _const_spec((1, LANES)), _const_spec((1, LANES))],
        out_specs=[_row_spec(tm, GD_QKW, tt), _row_spec(tm, GD_QKW, tt), _row_spec(tm, GD_VW, tt),
                   _row_spec(tm, GD_VW, tt), _row_spec(tm, LANES, tt)],
        out_shape=[jax.ShapeDtypeStruct((N, GD_QKW), BF16), jax.ShapeDtypeStruct((N, GD_QKW), BF16),
                   jax.ShapeDtypeStruct((N, GD_VW), BF16), jax.ShapeDtypeStruct((N, GD_VW), BF16),
                   jax.ShapeDtypeStruct((N, LANES), F32)],
        scratch_shapes=[pltpu.VMEM((SUBLANES, GD_QKV), F32), pltpu.VMEM((tm + SUBLANES, D_MODEL), F32)],
        compiler_params=_params(2),
        name=f"gdn_in_{layer}",
    )(x2, norm_g.reshape(1, D_MODEL), mod, w_main, wab_hi, wab_lo, conv_w.astype(F32),
      pad(a_log), pad(dt_bias))

    C = GD_CHUNK
    rows = GD_STEP_CHUNKS * C
    tc = T // rows
    tri = jnp.asarray(np.tril(np.ones((C, C), np.float32)), BF16)
    o = pl.pallas_call(
        _gd_scan_kernel,
        grid=(B, tc),
        in_specs=[_row_spec(rows, GD_QKW, tc), _row_spec(rows, GD_QKW, tc), _row_spec(rows, GD_VW, tc),
                  _row_spec(rows, LANES, tc), _const_spec((C, C)), _const_spec((1, GD_DV))],
        out_specs=_row_spec(rows, GD_VW, tc),
        out_shape=jax.ShapeDtypeStruct((N, GD_VW), BF16),
        scratch_shapes=[pltpu.VMEM((GD_V_HEADS, GD_DK, GD_DV), F32)],
        compiler_params=_params(2),
        name=f"gdn_scan_{layer}",
    )(q, k, v, gb, tri, onorm_g.reshape(1, GD_DV))
    return _out_proj(o, z, x2, mod, layer, w_out, B, T)


def kernel(x, c, positions, hgrn_lb, ada_w, ada_b, norm_g, hg_in_w, hg_out_w, hg_onorm, sw_in_w, sw_out_w,
           sw_qnorm, sw_knorm, sw_sinks, gd_in_w, gd_out_w, gd_conv_w, gd_a_log, gd_dt_bias, gd_onorm):
    B, T, _ = x.shape
    lb_all = jnp.cumsum(jax.nn.softmax(hgrn_lb.astype(F32), axis=0), axis=0)
    lb_all = lb_all - lb_all[0:1]
    mod = _ada_mod(c, ada_w, ada_b)
    x2 = x.reshape(B * T, D_MODEL)
    for i in range(DEPTH):
        j, kind = divmod(i, 3)
        if kind == 0:
            x2 = _hgrn2_layer(x2, mod, i, norm_g[i], lb_all[i], hg_in_w[j], hg_out_w[j], hg_onorm[j], B, T)
        elif kind == 1:
            x2 = _swa_layer(x2, mod, i, norm_g[i], positions, sw_in_w[j], sw_out_w[j], sw_qnorm[j],
                            sw_knorm[j], sw_sinks[j], B, T)
        else:
            x2 = _gdn_layer(x2, mod, i, norm_g[i], gd_in_w[j], gd_out_w[j], gd_conv_w[j], gd_a_log[j],
                            gd_dt_bias[j], gd_onorm[j], B, T)
    return x2.reshape(B, T, D_MODEL)
```
